```python
import math
import jax, jax.numpy as jnp
from jax import lax
import numpy as np

D_MODEL = 2048
BATCH = 2
SEQ = 4096
DEPTH = 4

D_INNER_A = D_MODEL
SSD_HEAD_DIM = 64
SSD_HEADS = D_INNER_A // SSD_HEAD_DIM
SSD_GROUPS = 4
SSD_STATE = 128
SSD_CHUNK = 128
CONV_WIDTH = 4
CONV_DIM = D_INNER_A + 2 * SSD_GROUPS * SSD_STATE
D_INNER_B = D_MODEL
SGU_CHUNK = 128
SGU_GROUP_DIM = 128
SGU_GROUPS = D_INNER_B // SGU_GROUP_DIM
D_INNER_C = D_MODEL
SB_HEAD_DIM = 128
SB_HEADS = D_INNER_C // SB_HEAD_DIM
SB_BLOCK = 128
EVEN_IN = 2 * D_INNER_A + 2 * SSD_GROUPS * SSD_STATE + SSD_HEADS + 3 * D_INNER_B
ODD_IN = 4 * D_INNER_C
N_EVEN = (DEPTH + 1) // 2
N_ODD = DEPTH // 2
NORM_EPS = 1e-5

kernel_name = "hybrid_ssd_sgu_stickbreak_trunk"


def rms_norm(x, w, eps=NORM_EPS):
    xf = x.astype(jnp.float32)
    xf = xf * lax.rsqrt(jnp.mean(xf * xf, axis=-1, keepdims=True) + eps)
    return xf.astype(x.dtype) * w


def layer_norm(x, w, b, eps=NORM_EPS):
    xf = x.astype(jnp.float32)
    mu = jnp.mean(xf, axis=-1, keepdims=True)
    xc = xf - mu
    xf = xc * lax.rsqrt(jnp.mean(xc * xc, axis=-1, keepdims=True) + eps)
    return xf.astype(x.dtype) * w + b


def causal_depthwise_conv(x, w, b):
    c = x.shape[-1]
    k = w.shape[0]
    y = lax.conv_general_dilated(
        x, w[:, None, :], window_strides=(1,), padding=((k - 1, 0),),
        dimension_numbers=("NWC", "WIO", "NWC"), feature_group_count=c)
    return y + b


def ssd_scan(xs, dt, b_in, c_in, a_log, d_skip):
    bsz, s, _ = xs.shape
    nc, L = s // SSD_CHUNK, SSD_CHUNK
    G, R, P, N = SSD_GROUPS, SSD_HEADS // SSD_GROUPS, SSD_HEAD_DIM, SSD_STATE
    f32 = jnp.float32
    a = -jnp.exp(a_log.astype(f32))
    x_h = xs.reshape(bsz, s, SSD_HEADS, P).astype(f32)
    x_c = (x_h * dt[..., None]).reshape(bsz, nc, L, G, R, P)
    b_c = b_in.astype(f32).reshape(bsz, nc, L, G, N)
    c_c = c_in.astype(f32).reshape(bsz, nc, L, G, N)
    a_dt = (dt * a).reshape(bsz, nc, L, G, R)
    cs = jnp.cumsum(a_dt, axis=2)
    causal = jnp.tril(jnp.ones((L, L), dtype=bool))[None, None, :, :, None, None]
    seg = cs[:, :, :, None] - cs[:, :, None, :]
    decay = jnp.exp(jnp.where(causal, seg, -jnp.inf))
    scores = jnp.einsum("bclgn,bcsgn->bclsg", c_c, b_c)
    y_diag = jnp.einsum("bclsg,bclsgr,bcsgrp->bclgrp", scores, decay, x_c)
    end_decay = jnp.exp(cs[:, :, -1:] - cs)
    states = jnp.einsum("bclgn,bclgr,bclgrp->bcgrpn", b_c, end_decay, x_c)
    chunk_decay = jnp.exp(cs[:, :, -1])

    def step(h, inp):
        st, dec = inp
        return dec[..., None, None] * h + st, h

    h0 = jnp.zeros((bsz, G, R, P, N), f32)
    _, prev = lax.scan(step, h0, (jnp.moveaxis(states, 1, 0), jnp.moveaxis(chunk_decay, 1, 0)))
    prev = jnp.moveaxis(prev, 0, 1)
    y_off = jnp.einsum("bclgn,bcgrpn,bclgr->bclgrp", c_c, prev, jnp.exp(cs))
    y = (y_diag + y_off).reshape(bsz, s, SSD_HEADS, P) + x_h * d_skip.astype(f32)[:, None]
    return y.reshape(bsz, s, D_INNER_A).astype(xs.dtype)


def chunked_sgu(u, v, ln_w, ln_b, w_s, b_s):
    bsz, s, _ = u.shape
    nc = s // SGU_CHUNK
    u = jax.nn.gelu(u)
    v = layer_norm(jax.nn.gelu(v), ln_w, ln_b)
    v_c = v.reshape(bsz, nc, SGU_CHUNK, SGU_GROUPS, SGU_GROUP_DIM)
    w_causal = jnp.tril(w_s)
    mixed = jnp.einsum("gts,bcsgd->bctgd", w_causal, v_c) + b_s.T[:, :, None]
    return u * mixed.reshape(bsz, s, D_INNER_B)


def stick_breaking_attention(q, k, v):
    bsz, s, _, d = q.shape
    scale = d ** -0.5
    outs = []
    for blk in range(s // SB_BLOCK):
        q0, q1 = blk * SB_BLOCK, (blk + 1) * SB_BLOCK
        qb = q[:, q0:q1].astype(jnp.float32)
        kb = k[:, :q1].astype(jnp.float32)
        z = jnp.einsum("bthd,bshd->bhts", qb, kb) * scale
        mask = jnp.arange(q1)[None, :] < jnp.arange(q0, q1)[:, None]
        log_keep = jnp.where(mask, -jax.nn.softplus(z), 0.0)
        after = lax.cumsum(log_keep, axis=3, reverse=True) - log_keep
        w = jnp.where(mask, jnp.exp(jax.nn.log_sigmoid(z) + after), 0.0)
        outs.append(jnp.einsum("bhts,bshd->bthd", w.astype(v.dtype), v[:, :q1]))
    return jnp.concatenate(outs, axis=1)


def even_layer(h, w_in, conv_w, conv_b, dt_bias, a_log, d_skip, ssd_norm_w,
               sgu_ln_w, sgu_ln_b, sgu_w, sgu_b, w_out):
    proj = h @ w_in
    o1 = D_INNER_A
    o2 = o1 + CONV_DIM
    o3 = o2 + SSD_HEADS
    o4 = o3 + D_INNER_B
    o5 = o4 + D_INNER_B
    z_a, xbc, dt_raw, g_b, u_b, v_b = jnp.split(proj, [o1, o2, o3, o4, o5], axis=-1)
    xbc = jax.nn.silu(causal_depthwise_conv(xbc, conv_w, conv_b))
    xs, b_in, c_in = jnp.split(xbc, [D_INNER_A, D_INNER_A + SSD_GROUPS * SSD_STATE], axis=-1)
    dt = jax.nn.softplus(dt_raw.astype(jnp.float32) + dt_bias.astype(jnp.float32))
    y_a = ssd_scan(xs, dt, b_in, c_in, a_log, d_skip)
    y_a = rms_norm(y_a * jax.nn.silu(z_a), ssd_norm_w)
    y_b = chunked_sgu(u_b, v_b, sgu_ln_w, sgu_ln_b, sgu_w, sgu_b) * jax.nn.silu(g_b)
    return jnp.concatenate([y_a, y_b], axis=-1) @ w_out


def odd_layer(h, w_in, w_out):
    bsz, s, _ = h.shape
    q, k, v, g = jnp.split(h @ w_in, 4, axis=-1)
    shp = (bsz, s, SB_HEADS, SB_HEAD_DIM)
    y = stick_breaking_attention(q.reshape(shp), k.reshape(shp), v.reshape(shp))
    return (y.reshape(bsz, s, D_INNER_C) * jax.nn.silu(g)) @ w_out


def setup_inputs(seed: int = 0) -> dict:
    key = jax.random.key(seed)
    ks = jax.random.split(key, 20)
    f32 = jnp.float32
    nrm = lambda k, shp, sc: jax.random.normal(k, shp, f32) * sc
    dt = jnp.exp(jax.random.uniform(ks[5], (N_EVEN, SSD_HEADS), f32)
                 * (math.log(0.1) - math.log(0.001)) + math.log(0.001))
    return {
        "x": jax.random.normal(ks[0], (BATCH, SEQ, D_MODEL), f32),
        "norm_w": 1.0 + nrm(ks[1], (DEPTH, D_MODEL), 0.05),
        "final_norm_w": 1.0 + nrm(ks[2], (D_MODEL,), 0.05),
        "ev_w_in": nrm(ks[3], (N_EVEN, D_MODEL, EVEN_IN), D_MODEL ** -0.5),
        "ev_conv_w": nrm(ks[4], (N_EVEN, CONV_WIDTH, CONV_DIM), CONV_WIDTH ** -0.5),
        "ev_conv_b": nrm(ks[6], (N_EVEN, CONV_DIM), 0.02),
        "ev_dt_bias": dt + jnp.log(-jnp.expm1(-dt)),
        "ev_a_log": jnp.log(jax.random.uniform(ks[7], (N_EVEN, SSD_HEADS), f32, 1.0, 16.0)),
        "ev_d_skip": 1.0 + nrm(ks[8], (N_EVEN, SSD_HEADS), 0.1),
        "ev_ssd_norm_w": 1.0 + nrm(ks[9], (N_EVEN, D_INNER_A), 0.05),
        "ev_sgu_ln_w": 1.0 + nrm(ks[10], (N_EVEN, D_INNER_B), 0.05),
        "ev_sgu_ln_b": nrm(ks[11], (N_EVEN, D_INNER_B), 0.02),
        "ev_sgu_w": nrm(ks[12], (N_EVEN, SGU_GROUPS, SGU_CHUNK, SGU_CHUNK), SGU_CHUNK ** -0.5),
        "ev_sgu_b": 1.0 + nrm(ks[13], (N_EVEN, SGU_GROUPS, SGU_CHUNK), 0.1),
        "ev_w_out": nrm(ks[14], (N_EVEN, D_INNER_A + D_INNER_B, D_MODEL), (D_INNER_A + D_INNER_B) ** -0.5),
        "od_w_in": nrm(ks[15], (N_ODD, D_MODEL, ODD_IN), D_MODEL ** -0.5),
        "od_w_out": nrm(ks[16], (N_ODD, D_INNER_C, D_MODEL), D_INNER_C ** -0.5),
    }


def reference(x, norm_w, final_norm_w, ev_w_in, ev_conv_w, ev_conv_b, ev_dt_bias,
              ev_a_log, ev_d_skip, ev_ssd_norm_w, ev_sgu_ln_w, ev_sgu_ln_b,
              ev_sgu_w, ev_sgu_b, ev_w_out, od_w_in, od_w_out):
    h = x
    for layer in range(DEPTH):
        hn = rms_norm(h, norm_w[layer])
        i = layer // 2
        if layer % 2 == 0:
            delta = even_layer(hn, ev_w_in[i], ev_conv_w[i], ev_conv_b[i], ev_dt_bias[i],
                               ev_a_log[i], ev_d_skip[i], ev_ssd_norm_w[i], ev_sgu_ln_w[i],
                               ev_sgu_ln_b[i], ev_sgu_w[i], ev_sgu_b[i], ev_w_out[i])
        else:
            delta = odd_layer(hn, od_w_in[i], od_w_out[i])
        h = h + delta
    return rms_norm(h, final_norm_w)
```

```python
import functools
import math

import jax
import jax.numpy as jnp
from jax import lax
from jax.experimental import pallas as pl
from jax.experimental.pallas import tpu as pltpu

F32 = jnp.float32
BF16 = jnp.bfloat16

D_MODEL = 2048
NORM_EPS = 1e-5
SSD_HEAD_DIM = 64
SSD_HEADS = 32
SSD_GROUPS = 4
SSD_STATE = 128
CHUNK = 128
CONV_WIDTH = 4
D_INNER = 2048
BC_DIM = 2 * SSD_GROUPS * SSD_STATE
GROUP_COLS = D_INNER // SSD_GROUPS
SGU_GROUPS = 16
SGU_GROUP_DIM = 128
SB_HEADS = 16
SB_HEAD_DIM = 128
ATTN_TQ = 256
ATTN_TK = 128
DT_LANES = 128
DT_COPIES = 3

V7X_VMEM_LIMIT = 48 * 1024 * 1024


def _params(sem):
    return pltpu.CompilerParams(dimension_semantics=sem, vmem_limit_bytes=V7X_VMEM_LIMIT)


def _sigmoid(x):
    return 1.0 / (1.0 + jnp.exp(-x))


def _silu(x):
    return x * _sigmoid(x)


def _gelu_tanh(x):
    c = math.sqrt(2.0 / math.pi)
    return x * (0.5 * (1.0 + jnp.tanh(c * (x + 0.044715 * (x * x * x)))))


def _split3(x):
    h1 = x.astype(BF16)
    r1 = x - h1.astype(F32)
    h2 = r1.astype(BF16)
    r2 = r1 - h2.astype(F32)
    return h1, h2, r2.astype(BF16)


def _norm_matmul_kernel(x_ref, nw_ref, w_ref, o_ref, hn_ref):
    @pl.when(pl.program_id(1) == 0)
    def _():
        x = x_ref[...]
        ms = jnp.mean(x * x, axis=-1, keepdims=True)
        hn_ref[...] = (x * lax.rsqrt(ms + NORM_EPS) * nw_ref[...]).astype(BF16)

    o_ref[...] = jnp.dot(hn_ref[...], w_ref[...],
                         preferred_element_type=F32).astype(o_ref.dtype)


def _norm_matmul(h, norm_w, w, *, tm, tn, out_dtype):
    m, d = h.shape
    n = w.shape[1]
    return pl.pallas_call(
        _norm_matmul_kernel,
        grid=(m // tm, n // tn),
        in_specs=[
            pl.BlockSpec((tm, d), lambda i, j: (i, 0)),
            pl.BlockSpec((1, d), lambda i, j: (0, 0)),
            pl.BlockSpec((d, tn), lambda i, j: (0, j)),
        ],
        out_specs=pl.BlockSpec((tm, tn), lambda i, j: (i, j)),
        out_shape=jax.ShapeDtypeStruct((m, n), out_dtype),
        scratch_shapes=[pltpu.VMEM((tm, d), BF16)],
        compiler_params=_params(("parallel", "arbitrary")),
        name="norm_in_proj",
    )(h, norm_w.reshape(1, d), w)


def _out_proj_kernel(*refs, n_in, final_norm):
    y_refs = refs[:n_in]
    w_refs = refs[n_in:2 * n_in]
    h_ref = refs[2 * n_in]
    o_ref = refs[-1]
    acc = h_ref[...]
    for y_ref, w_ref in zip(y_refs, w_refs):
        acc = acc + jnp.dot(y_ref[...], w_ref[...], preferred_element_type=F32)
    if final_norm:
        fw_ref = refs[2 * n_in + 1]
        ms = jnp.mean(acc * acc, axis=-1, keepdims=True)
        acc = acc * lax.rsqrt(ms + NORM_EPS) * fw_ref[...]
    o_ref[...] = acc


def _out_proj(ys, ws, h, final_w, *, tm, tn):
    m, n = h.shape
    n_in = len(ys)
    final_norm = final_w is not None
    if final_norm:
        assert tn == n
    in_specs = [pl.BlockSpec((tm, y.shape[1]), lambda i, j: (i, 0)) for y in ys]
    in_specs += [pl.BlockSpec((w.shape[0], tn), lambda i, j: (0, j)) for w in ws]
    in_specs += [pl.BlockSpec((tm, tn), lambda i, j: (i, j))]
    args = list(ys) + list(ws) + [h]
    if final_norm:
        in_specs += [pl.BlockSpec((1, n), lambda i, j: (0, 0))]
        args += [final_w.reshape(1, n)]
    return pl.pallas_call(
        functools.partial(_out_proj_kernel, n_in=n_in, final_norm=final_norm),
        grid=(m // tm, n // tn),
        in_specs=in_specs,
        out_specs=pl.BlockSpec((tm, tn), lambda i, j: (i, j)),
        out_shape=jax.ShapeDtypeStruct((m, n), F32),
        input_output_aliases={2 * n_in: 0},
        compiler_params=_params(("parallel", "parallel")),
        name="out_proj",
    )(*args)


def _causal_conv(ext_ref, tail_ref, cur, w_ref, b_ref):
    ext_ref[0:8, :] = tail_ref[...]
    ext_ref[8:8 + CHUNK, :] = cur
    tail_ref[...] = cur[CHUNK - 8:, :]
    acc = b_ref[...] + w_ref[CONV_WIDTH - 1:CONV_WIDTH, :] * cur
    for k in range(CONV_WIDTH - 1):
        d = CONV_WIDTH - 1 - k
        acc = acc + w_ref[k:k + 1, :] * ext_ref[8 - d:8 - d + CHUNK, :]
    return acc


def _ssd_kernel(z_ref, x_ref, bc_ref, dt_ref, cwx_ref, cbx_ref, cwbc_ref, cbbc_ref,
                dtb_ref, alog_ref, dskip_ref, nw_ref, e64_ref, e128_ref,
                o_ref,
                state_ref, xtail_ref, bctail_ref, xext_ref, bcext_ref, y_ref):
    c = pl.program_id(1)

    @pl.when(c == 0)
    def _():
        state_ref[...] = jnp.zeros_like(state_ref)
        xtail_ref[...] = jnp.zeros_like(xtail_ref)
        bctail_ref[...] = jnp.zeros_like(bctail_ref)

    L = CHUNK
    row = lax.broadcasted_iota(jnp.int32, (L, L), 0)
    col = lax.broadcasted_iota(jnp.int32, (L, L), 1)
    causal = row >= col
    tri = jnp.where(causal, 1.0, 0.0).astype(BF16)
    lane = lax.broadcasted_iota(jnp.int32, (L, DT_LANES), 1)

    dt_in = dt_ref[...] + dtb_ref[...]
    dt = jnp.maximum(dt_in, 0.0) + jnp.log1p(jnp.exp(-jnp.abs(dt_in)))
    a_dt = dt * (-jnp.exp(alog_ref[...]))
    p1, p2, p3 = _split3(a_dt)
    cs = (jnp.dot(tri, p1, preferred_element_type=F32)
          + jnp.dot(tri, p2, preferred_element_type=F32)
          + jnp.dot(tri, p3, preferred_element_type=F32))
    cs_t = cs.T

    def lane_pieces(v):
        q1, q2, q3 = _split3(v)
        return jnp.where(lane < SSD_HEADS, q1, jnp.where(lane < 2 * SSD_HEADS, q2, q3))

    dt_p = lane_pieces(dt)
    cs_p = lane_pieces(cs)
    e64 = e64_ref[...]
    dt_e = jnp.dot(dt_p, e64, preferred_element_type=F32)
    cs_e = jnp.dot(cs_p, e64, preferred_element_type=F32)
    cs_col = jnp.dot(cs_p, e128_ref[...], preferred_element_type=F32)

    xs = _silu(_causal_conv(xext_ref, xtail_ref, x_ref[...].astype(F32), cwx_ref, cbx_ref))
    bc = _silu(_causal_conv(bcext_ref, bctail_ref, bc_ref[...].astype(F32), cwbc_ref, cbbc_ref))

    cs_last = cs_e[L - 1:L, :]
    x_dt = xs * dt_e
    x_dt_b = x_dt.astype(BF16)
    x_end_b = (x_dt * jnp.exp(cs_last - cs_e)).astype(BF16)
    in_decay = jnp.exp(cs_e)
    chunk_decay = jnp.exp(cs_last)

    lane_l = lax.broadcasted_iota(jnp.int32, (L, 2 * SSD_HEAD_DIM), 1)
    first_head = lane_l < SSD_HEAD_DIM
    ssq = jnp.zeros((L, 1), F32)
    for g in range(SSD_GROUPS):
        gs = slice(g * GROUP_COLS, (g + 1) * GROUP_COLS)
        b_g = bc[:, g * SSD_STATE:(g + 1) * SSD_STATE].astype(BF16)
        c_g = bc[:, (SSD_GROUPS + g) * SSD_STATE:(SSD_GROUPS + g + 1) * SSD_STATE].astype(BF16)
        scores = lax.dot_general(c_g, b_g, (((1,), (1,)), ((), ())),
                                 preferred_element_type=F32)
        heads_per_group = SSD_HEADS // SSD_GROUPS
        y_parts = []
        for pair in range(heads_per_group // 2):
            ms = []
            for r in range(2):
                h = g * heads_per_group + 2 * pair + r
                seg = cs_col[:, h * L:(h + 1) * L] - cs_t[h:h + 1, :]
                decay = jnp.exp(jnp.where(causal, seg, -jnp.inf))
                ms.append((scores * decay).astype(BF16))
            lhs = jnp.concatenate(ms, axis=1)
            c0 = g * GROUP_COLS + pair * 2 * SSD_HEAD_DIM
            xp = x_dt_b[:, c0:c0 + 2 * SSD_HEAD_DIM]
            zero = jnp.zeros_like(xp)
            rhs = jnp.concatenate([jnp.where(first_head, xp, zero),
                                   jnp.where(first_head, zero, xp)], axis=0)
            y_parts.append(jnp.dot(lhs, rhs, preferred_element_type=F32))
        y_diag = jnp.concatenate(y_parts, axis=1)

        st = state_ref[:, gs]
        y_off = jnp.dot(c_g, st.astype(BF16), preferred_element_type=F32) * in_decay[:, gs]
        new_st = lax.dot_general(b_g, x_end_b[:, gs], (((0,), (0,)), ((), ())),
                                 preferred_element_type=F32)
        state_ref[:, gs] = chunk_decay[:, gs] * st + new_st

        y = y_diag + y_off + xs[:, gs] * dskip_ref[:, gs]
        y = y * _silu(z_ref[:, gs].astype(F32))
        ssq = ssq + jnp.sum(y * y, axis=-1, keepdims=True)
        y_ref[:, gs] = y

    inv = lax.rsqrt(ssq * (1.0 / D_INNER) + NORM_EPS)
    o_ref[...] = (y_ref[...] * inv * nw_ref[...]).astype(o_ref.dtype)


def _ssd_mixer(proj, dt_raw, conv_w, conv_b, dt_bias, a_log, d_skip, norm_w, *, batch, seq):
    nc = seq // CHUNK
    t = batch * seq
    lanes = jnp.arange(DT_LANES)
    valid = lanes < DT_COPIES * SSD_HEADS
    head_of_lane = lanes % SSD_HEADS

    def tile_heads(v):
        return jnp.where(valid, v[head_of_lane], 0.0).reshape(1, DT_LANES).astype(F32)

    def expand(width):
        cols = jnp.arange(SSD_HEADS * width) // width
        return ((head_of_lane[:, None] == cols[None, :]) & valid[:, None]).astype(BF16)

    e64 = expand(SSD_HEAD_DIM)
    e128 = expand(CHUNK)
    d_skip_e = jnp.repeat(d_skip, SSD_HEAD_DIM).reshape(1, D_INNER)

    tok = lambda b, c: b * nc + c
    const = lambda b, c: (0, 0)
    full = lambda a: pl.BlockSpec(a.shape, const)
    cwx, cwbc = conv_w[:, :D_INNER], conv_w[:, D_INNER:]
    cbx, cbbc = conv_b[:D_INNER].reshape(1, -1), conv_b[D_INNER:].reshape(1, -1)
    dtb, alog = tile_heads(dt_bias), tile_heads(a_log)
    nw = norm_w.reshape(1, D_INNER)
    params = [cwx, cbx, cwbc, cbbc, dtb, alog, d_skip_e, nw, e64, e128]
    return pl.pallas_call(
        _ssd_kernel,
        grid=(batch, nc),
        in_specs=[
            pl.BlockSpec((CHUNK, D_INNER), lambda b, c: (tok(b, c), 0)),
            pl.BlockSpec((CHUNK, D_INNER), lambda b, c: (tok(b, c), 4)),
            pl.BlockSpec((CHUNK, BC_DIM), lambda b, c: (tok(b, c), 10)),
            pl.BlockSpec((CHUNK, DT_LANES), lambda b, c: (tok(b, c), 0)),
        ] + [full(p) for p in params],
        out_specs=pl.BlockSpec((CHUNK, D_INNER), lambda b, c: (tok(b, c), 0)),
        out_shape=jax.ShapeDtypeStruct((t, D_INNER), BF16),
        scratch_shapes=[
            pltpu.VMEM((SSD_STATE, D_INNER), F32),
            pltpu.VMEM((8, D_INNER), F32),
            pltpu.VMEM((8, BC_DIM), F32),
            pltpu.VMEM((8 + CHUNK, D_INNER), F32),
            pltpu.VMEM((8 + CHUNK, BC_DIM), F32),
            pltpu.VMEM((CHUNK, D_INNER), F32),
        ],
        compiler_params=_params(("parallel", "arbitrary")),
        name="ssd_mixer",
    )(proj, proj, proj, dt_raw, *params)


def _sgu_kernel(g_ref, u_ref, v_ref, lnw_ref, lnb_ref, ws_ref, bias_ref, o_ref, wtri_ref):
    @pl.when(pl.program_id(0) == 0)
    def _():
        row = lax.broadcasted_iota(jnp.int32, (CHUNK, CHUNK), 0)
        col = lax.broadcasted_iota(jnp.int32, (CHUNK, CHUNK), 1)
        for g in range(SGU_GROUPS):
            wtri_ref[g] = jnp.where(row >= col, ws_ref[g], 0.0).astype(BF16)

    v = _gelu_tanh(v_ref[...].astype(F32))
    mu = jnp.mean(v, axis=-1, keepdims=True)
    vc = v - mu
    var = jnp.mean(vc * vc, axis=-1, keepdims=True)
    vn = (vc * lax.rsqrt(var + NORM_EPS) * lnw_ref[...] + lnb_ref[...]).astype(BF16)
    for g in range(SGU_GROUPS):
        gs = slice(g * SGU_GROUP_DIM, (g + 1) * SGU_GROUP_DIM)
        mixed = jnp.dot(wtri_ref[g], vn[:, gs], preferred_element_type=F32) + bias_ref[:, gs]
        u = _gelu_tanh(u_ref[:, gs].astype(F32))
        o_ref[:, gs] = (u * mixed * _silu(g_ref[:, gs].astype(F32))).astype(o_ref.dtype)


def _sgu_mixer(proj, ln_w, ln_b, w_s, b_s):
    t = proj.shape[0]
    bias = jnp.repeat(b_s.T, SGU_GROUP_DIM, axis=1)
    const2 = lambda i: (0, 0)
    return pl.pallas_call(
        _sgu_kernel,
        grid=(t // CHUNK,),
        in_specs=[
            pl.BlockSpec((CHUNK, D_INNER), lambda i: (i, 1)),
            pl.BlockSpec((CHUNK, D_INNER), lambda i: (i, 2)),
            pl.BlockSpec((CHUNK, D_INNER), lambda i: (i, 3)),
            pl.BlockSpec((1, D_INNER), const2),
            pl.BlockSpec((1, D_INNER), const2),
            pl.BlockSpec(w_s.shape, lambda i: (0, 0, 0)),
            pl.BlockSpec((CHUNK, D_INNER), const2),
        ],
        out_specs=pl.BlockSpec((CHUNK, D_INNER), lambda i: (i, 0)),
        out_shape=jax.ShapeDtypeStruct((t, D_INNER), BF16),
        scratch_shapes=[pltpu.VMEM((SGU_GROUPS, CHUNK, CHUNK), BF16)],
        compiler_params=_params(("arbitrary",)),
        name="sgu_mixer",
    )(proj, proj, proj, ln_w.reshape(1, -1), ln_b.reshape(1, -1), w_s, bias)


def _attn_kernel(q_ref, k_ref, v_ref, g_ref, o_ref):
    qi = pl.program_id(2)
    tq, tk = ATTN_TQ, ATTN_TK
    q = q_ref[...]
    srow = lax.broadcasted_iota(jnp.int32, (tk, tk), 0)
    scol = lax.broadcasted_iota(jnp.int32, (tk, tk), 1)
    later = jnp.where(scol > srow, 1.0, 0.0).astype(BF16)
    key_off = lax.broadcasted_iota(jnp.int32, (tk, tq), 0)
    qry_off = lax.broadcasted_iota(jnp.int32, (tk, tq), 1)

    def block(j, carry, acc, masked):
        ks = pl.multiple_of(j * tk, tk)
        kj = k_ref[pl.ds(ks, tk), :]
        vj = v_ref[pl.ds(ks, tk), :]
        z = lax.dot_general(kj, q, (((1,), (1,)), ((), ())),
                            preferred_element_type=F32)
        sp = jnp.maximum(z, 0.0) + jnp.log(1.0 + jnp.exp(-jnp.abs(z)))
        if masked:
            m = (j * tk + key_off) < (qi * tq + qry_off)
            sp = jnp.where(m, sp, 0.0)
        hi = sp.astype(BF16)
        lo = (sp - hi.astype(F32)).astype(BF16)
        aft = (jnp.dot(later, hi, preferred_element_type=F32)
               + jnp.dot(later, lo, preferred_element_type=F32))
        w = jnp.exp((z - sp) - aft - carry)
        if masked:
            w = jnp.where(m, w, 0.0)
        acc = acc + lax.dot_general(vj, w.astype(BF16), (((0,), (0,)), ((), ())),
                                    preferred_element_type=F32)
        carry = carry + aft[0:1, :] + sp[0:1, :]
        return carry, acc

    carry = jnp.zeros((1, tq), F32)
    acc = jnp.zeros((SB_HEAD_DIM, tq), F32)
    n_diag = tq // tk
    for r in range(n_diag):
        carry, acc = block(qi * n_diag + (n_diag - 1 - r), carry, acc, True)

    def body(i, ca):
        return block(qi * n_diag - 1 - i, ca[0], ca[1], False)

    carry, acc = lax.fori_loop(0, qi * n_diag, body, (carry, acc))
    o_ref[...] = (acc.T * _silu(g_ref[...].astype(F32))).astype(o_ref.dtype)


def _attention(qkvg, *, batch, seq):
    nq = seq // ATTN_TQ
    t = batch * seq
    return pl.pallas_call(
        _attn_kernel,
        grid=(batch, SB_HEADS, nq),
        in_specs=[
            pl.BlockSpec((ATTN_TQ, SB_HEAD_DIM), lambda b, h, i: (b * nq + i, h)),
            pl.BlockSpec((seq, SB_HEAD_DIM), lambda b, h, i: (b, SB_HEADS + h)),
            pl.BlockSpec((seq, SB_HEAD_DIM), lambda b, h, i: (b, 2 * SB_HEADS + h)),
            pl.BlockSpec((ATTN_TQ, SB_HEAD_DIM), lambda b, h, i: (b * nq + i, 3 * SB_HEADS + h)),
        ],
        out_specs=pl.BlockSpec((ATTN_TQ, SB_HEAD_DIM), lambda b, h, i: (b * nq + i, h)),
        out_shape=jax.ShapeDtypeStruct((t, SB_HEADS * SB_HEAD_DIM), BF16),
        compiler_params=_params(("parallel", "parallel", "arbitrary")),
        name="stickbreak_attn",
    )(qkvg, qkvg, qkvg, qkvg)


def _even_layer(h, norm_w, w_in, conv_w, conv_b, dt_bias, a_log, d_skip, ssd_norm_w,
                sgu_ln_w, sgu_ln_b, sgu_w, sgu_b, w_out, *, batch, seq):
    o1 = D_INNER
    o2 = o1 + D_INNER + BC_DIM
    o3 = o2 + SSD_HEADS
    w_main = jnp.concatenate(
        [w_in[:, :o1], w_in[:, o3:], w_in[:, o1:o2]], axis=1).astype(BF16)
    w_dt = w_in[:, o2:o3]
    w_dt = jnp.concatenate(
        [w_dt] * DT_COPIES
        + [jnp.zeros((D_MODEL, DT_LANES - DT_COPIES * SSD_HEADS), F32)], axis=1).astype(BF16)
    proj = _norm_matmul(h, norm_w, w_main, tm=1024, tn=1024, out_dtype=BF16)
    dt_raw = _norm_matmul(h, norm_w, w_dt, tm=1024, tn=DT_LANES, out_dtype=F32)
    y_a = _ssd_mixer(proj, dt_raw, conv_w, conv_b, dt_bias, a_log, d_skip, ssd_norm_w,
                     batch=batch, seq=seq)
    y_b = _sgu_mixer(proj, sgu_ln_w, sgu_ln_b, sgu_w, sgu_b)
    w_out = w_out.astype(BF16)
    return _out_proj([y_a, y_b], [w_out[:D_INNER], w_out[D_INNER:]], h, None, tm=512, tn=1024)


def _odd_layer(h, norm_w, w_in, w_out, final_w, *, batch, seq):
    scale = SB_HEAD_DIM ** -0.5
    w = jnp.concatenate([w_in[:, :D_INNER] * scale, w_in[:, D_INNER:]], axis=1).astype(BF16)
    qkvg = _norm_matmul(h, norm_w, w, tm=1024, tn=1024, out_dtype=BF16)
    y = _attention(qkvg, batch=batch, seq=seq)
    tn = D_MODEL if final_w is not None else 1024
    return _out_proj([y], [w_out.astype(BF16)], h, final_w, tm=512, tn=tn)


def kernel(x, norm_w, final_norm_w, ev_w_in, ev_conv_w, ev_conv_b, ev_dt_bias, ev_a_log,
           ev_d_skip, ev_ssd_norm_w, ev_sgu_ln_w, ev_sgu_ln_b, ev_sgu_w, ev_sgu_b, ev_w_out,
           od_w_in, od_w_out):
    batch, seq, d = x.shape
    depth = norm_w.shape[0]
    h = x.reshape(batch * seq, d)
    for layer in range(depth):
        i = layer // 2
        if layer % 2 == 0:
            h = _even_layer(h, norm_w[layer], ev_w_in[i], ev_conv_w[i], ev_conv_b[i],
                            ev_dt_bias[i], ev_a_log[i], ev_d_skip[i], ev_ssd_norm_w[i],
                            ev_sgu_ln_w[i], ev_sgu_ln_b[i], ev_sgu_w[i], ev_sgu_b[i],
                            ev_w_out[i], batch=batch, seq=seq)
        else:
            fw = final_norm_w if layer == depth - 1 else None
            h = _odd_layer(h, norm_w[layer], od_w_in[i], od_w_out[i], fw, batch=batch, seq=seq)
    if depth % 2 == 1:
        raise NotImplementedError("final RMSNorm is fused into an odd last layer")
    return h.reshape(batch, seq, d)
```

```python
import functools
import math

import jax
import jax.numpy as jnp
from jax import lax
from jax.experimental import pallas as pl
from jax.experimental.pallas import tpu as pltpu

F32 = jnp.float32
BF16 = jnp.bfloat16

D_MODEL = 2048
NORM_EPS = 1e-5
SSD_HEAD_DIM = 64
SSD_HEADS = 32
SSD_GROUPS = 4
SSD_STATE = 128
CHUNK = 128
CONV_WIDTH = 4
D_INNER = 2048
BC_DIM = 2 * SSD_GROUPS * SSD_STATE
GROUP_COLS = D_INNER // SSD_GROUPS
SGU_GROUPS = 16
SGU_GROUP_DIM = 128
SB_HEADS = 16
SB_HEAD_DIM = 128
ATTN_TQ = 256
ATTN_TK = 128
ATTN_BAND = 2 * ATTN_TQ
ATTN_GROUP = 4
ATTN_SKIP_LOG2 = 160.0
DT_LANES = 128
DT_COPIES = 3

V7X_VMEM_LIMIT = 48 * 1024 * 1024


def _params(sem):
    return pltpu.CompilerParams(dimension_semantics=sem, vmem_limit_bytes=V7X_VMEM_LIMIT)


def _sigmoid(x):
    return 1.0 / (1.0 + jnp.exp(-x))


def _silu(x):
    return x * _sigmoid(x)


def _gelu_tanh(x):
    c = math.sqrt(2.0 / math.pi)
    return x * (0.5 * (1.0 + jnp.tanh(c * (x + 0.044715 * (x * x * x)))))


def _split3(x):
    h1 = x.astype(BF16)
    r1 = x - h1.astype(F32)
    h2 = r1.astype(BF16)
    r2 = r1 - h2.astype(F32)
    return h1, h2, r2.astype(BF16)


def _in_proj_kernel(*refs, segments, scaled_tiles, scale, with_dt):
    n_w = len(segments)
    x_ref, nw_ref = refs[:2]
    w_refs = refs[2:2 + n_w]
    hn_ref = refs[-1]
    if with_dt:
        wdt_ref, o_ref, dt_ref = refs[2 + n_w:5 + n_w]
    else:
        o_ref = refs[2 + n_w]
    j = pl.program_id(1)

    @pl.when(j == 0)
    def _():
        x = x_ref[...]
        ms = jnp.mean(x * x, axis=-1, keepdims=True)
        hn_ref[...] = (x * lax.rsqrt(ms + NORM_EPS) * nw_ref[...]).astype(BF16)
        if with_dt:
            dt_ref[...] = jnp.dot(hn_ref[...], wdt_ref[...], preferred_element_type=F32)

    def project(w_ref):
        res = jnp.dot(hn_ref[...], w_ref[...].astype(BF16), preferred_element_type=F32)
        if scaled_tiles:
            res = res * jnp.where(j < scaled_tiles, scale, 1.0)
        o_ref[...] = res.astype(o_ref.dtype)

    if n_w == 1:
        project(w_refs[0])
    else:
        for w_ref, (first, last) in zip(w_refs, segments):
            pl.when(jnp.logical_and(j >= first, j < last))(functools.partial(project, w_ref))


def _in_proj(h, norm_w, weights, *, tm, tn, out_tiles=None, scaled_tiles=0, scale=1.0, w_dt=None):
    m, d = h.shape
    segments, first = [], 0
    for _, _, n_tiles in weights:
        segments.append((first, first + n_tiles))
        first += n_tiles
    n_total = first

    def w_spec(layer, seg):
        lo, hi = seg
        return pl.BlockSpec((None, d, tn), lambda i, j: (layer, 0, jnp.clip(j - lo, 0, hi - lo - 1)))

    def out_tile(j):
        if out_tiles is None:
            return j
        return sum(jnp.where(jnp.logical_and(j >= lo, j < lo + cnt), j - lo + dst, 0)
                   for lo, cnt, dst in out_tiles)

    in_specs = [pl.BlockSpec((tm, d), lambda i, j: (i, 0)),
                pl.BlockSpec((1, d), lambda i, j: (0, 0))]
    in_specs += [w_spec(layer, seg) for (_, layer, _), seg in zip(weights, segments)]
    args = [h, norm_w.reshape(1, d)] + [w for w, _, _ in weights]
    out_specs = pl.BlockSpec((tm, tn), lambda i, j: (i, out_tile(j)))
    out_shape = jax.ShapeDtypeStruct((m, n_total * tn), BF16)
    if w_dt is not None:
        in_specs.append(pl.BlockSpec(w_dt.shape, lambda i, j: (0, 0)))
        args.append(w_dt)
        out_specs = [out_specs, pl.BlockSpec((tm, DT_LANES), lambda i, j: (i, 0))]
        out_shape = [out_shape, jax.ShapeDtypeStruct((m, DT_LANES), F32)]
    return pl.pallas_call(
        functools.partial(_in_proj_kernel, segments=tuple(segments), scaled_tiles=scaled_tiles,
                          scale=scale, with_dt=w_dt is not None),
        grid=(m // tm, n_total),
        in_specs=in_specs,
        out_specs=out_specs,
        out_shape=out_shape,
        scratch_shapes=[pltpu.VMEM((tm, d), BF16)],
        compiler_params=_params(("parallel", "arbitrary")),
        name="norm_in_proj",
    )(*args)


def _out_proj_kernel(*refs, n_in, final_norm):
    y_refs = refs[:n_in]
    w_refs = refs[n_in:2 * n_in]
    h_ref = refs[2 * n_in]
    o_ref = refs[-1]
    acc = h_ref[...]
    for y_ref, w_ref in zip(y_refs, w_refs):
        acc = acc + jnp.dot(y_ref[...], w_ref[...].astype(BF16), preferred_element_type=F32)
    if final_norm:
        fw_ref = refs[2 * n_in + 1]
        ms = jnp.mean(acc * acc, axis=-1, keepdims=True)
        acc = acc * lax.rsqrt(ms + NORM_EPS) * fw_ref[...]
    o_ref[...] = acc


def _out_proj(ys, ws, h, final_w, *, tm, tn, in_place):
    m, n = h.shape
    n_in = len(ys)
    final_norm = final_w is not None
    if final_norm:
        assert tn == n

    def w_spec(kdim, layer, row_block):
        return pl.BlockSpec((None, kdim, tn), lambda i, j: (layer, row_block, j))

    in_specs = [pl.BlockSpec((tm, y.shape[1]), lambda i, j: (i, 0)) for y in ys]
    in_specs += [w_spec(y.shape[1], layer, rb) for y, (_, layer, rb) in zip(ys, ws)]
    in_specs += [pl.BlockSpec((tm, tn), lambda i, j: (i, j))]
    args = list(ys) + [w for w, _, _ in ws] + [h]
    if final_norm:
        in_specs += [pl.BlockSpec((1, n), lambda i, j: (0, 0))]
        args += [final_w.reshape(1, n)]
    return pl.pallas_call(
        functools.partial(_out_proj_kernel, n_in=n_in, final_norm=final_norm),
        grid=(m // tm, n // tn),
        in_specs=in_specs,
        out_specs=pl.BlockSpec((tm, tn), lambda i, j: (i, j)),
        out_shape=jax.ShapeDtypeStruct((m, n), F32),
        input_output_aliases={2 * n_in: 0} if in_place else {},
        compiler_params=_params(("parallel", "parallel")),
        name="out_proj",
    )(*args)


def _causal_conv(ext_ref, tail_ref, cur, w_ref, b_ref):
    ext_ref[0:8, :] = tail_ref[...]
    ext_ref[8:8 + CHUNK, :] = cur
    tail_ref[...] = cur[CHUNK - 8:, :]
    acc = b_ref[...] + w_ref[CONV_WIDTH - 1:CONV_WIDTH, :] * cur
    for k in range(CONV_WIDTH - 1):
        d = CONV_WIDTH - 1 - k
        acc = acc + w_ref[k:k + 1, :] * ext_ref[8 - d:8 - d + CHUNK, :]
    return acc


def _ssd_kernel(z_ref, x_ref, bc_ref, dt_ref, cwx_ref, cbx_ref, cwbc_ref, cbbc_ref,
                dtb_ref, alog_ref, dskip_ref, nw_ref, e64_ref, e128_ref,
                o_ref,
                state_ref, xtail_ref, bctail_ref, xext_ref, bcext_ref, y_ref):
    c = pl.program_id(1)

    @pl.when(c == 0)
    def _():
        state_ref[...] = jnp.zeros_like(state_ref)
        xtail_ref[...] = jnp.zeros_like(xtail_ref)
        bctail_ref[...] = jnp.zeros_like(bctail_ref)

    L = CHUNK
    row = lax.broadcasted_iota(jnp.int32, (L, L), 0)
    col = lax.broadcasted_iota(jnp.int32, (L, L), 1)
    causal = row >= col
    tri = jnp.where(causal, 1.0, 0.0).astype(BF16)
    lane = lax.broadcasted_iota(jnp.int32, (L, DT_LANES), 1)

    dt_in = dt_ref[...] + dtb_ref[...]
    dt = jnp.maximum(dt_in, 0.0) + jnp.log1p(jnp.exp(-jnp.abs(dt_in)))
    a_dt = dt * (-jnp.exp(alog_ref[...]))
    p1, p2, p3 = _split3(a_dt)
    cs = (jnp.dot(tri, p1, preferred_element_type=F32)
          + jnp.dot(tri, p2, preferred_element_type=F32)
          + jnp.dot(tri, p3, preferred_element_type=F32))
    cs_t = cs.T

    def lane_pieces(v):
        q1, q2, q3 = _split3(v)
        return jnp.where(lane < SSD_HEADS, q1, jnp.where(lane < 2 * SSD_HEADS, q2, q3))

    dt_p = lane_pieces(dt)
    cs_p = lane_pieces(cs)
    e64 = e64_ref[...]
    dt_e = jnp.dot(dt_p, e64, preferred_element_type=F32)
    cs_e = jnp.dot(cs_p, e64, preferred_element_type=F32)
    cs_col = jnp.dot(cs_p, e128_ref[...], preferred_element_type=F32)

    xs = _silu(_causal_conv(xext_ref, xtail_ref, x_ref[...].astype(F32), cwx_ref, cbx_ref))
    bc = _silu(_causal_conv(bcext_ref, bctail_ref, bc_ref[...].astype(F32), cwbc_ref, cbbc_ref))

    cs_last = cs_e[L - 1:L, :]
    x_dt = xs * dt_e
    x_dt_b = x_dt.astype(BF16)
    x_end_b = (x_dt * jnp.exp(cs_last - cs_e)).astype(BF16)
    in_decay = jnp.exp(cs_e)
    chunk_decay = jnp.exp(cs_last)

    lane_l = lax.broadcasted_iota(jnp.int32, (L, 2 * SSD_HEAD_DIM), 1)
    first_head = lane_l < SSD_HEAD_DIM
    ssq = jnp.zeros((L, 1), F32)
    for g in range(SSD_GROUPS):
        gs = slice(g * GROUP_COLS, (g + 1) * GROUP_COLS)
        b_g = bc[:, g * SSD_STATE:(g + 1) * SSD_STATE].astype(BF16)
        c_g = bc[:, (SSD_GROUPS + g) * SSD_STATE:(SSD_GROUPS + g + 1) * SSD_STATE].astype(BF16)
        scores = lax.dot_general(c_g, b_g, (((1,), (1,)), ((), ())),
                                 preferred_element_type=F32)
        heads_per_group = SSD_HEADS // SSD_GROUPS
        y_parts = []
        for pair in range(heads_per_group // 2):
            ms = []
            for r in range(2):
                h = g * heads_per_group + 2 * pair + r
                seg = cs_col[:, h * L:(h + 1) * L] - cs_t[h:h + 1, :]
                decay = jnp.exp(jnp.where(causal, seg, -jnp.inf))
                ms.append((scores * decay).astype(BF16))
            lhs = jnp.concatenate(ms, axis=1)
            c0 = g * GROUP_COLS + pair * 2 * SSD_HEAD_DIM
            xp = x_dt_b[:, c0:c0 + 2 * SSD_HEAD_DIM]
            zero = jnp.zeros_like(xp)
            rhs = jnp.concatenate([jnp.where(first_head, xp, zero),
                                   jnp.where(first_head, zero, xp)], axis=0)
            y_parts.append(jnp.dot(lhs, rhs, preferred_element_type=F32))
        y_diag = jnp.concatenate(y_parts, axis=1)

        st = state_ref[:, gs]
        y_off = jnp.dot(c_g, st.astype(BF16), preferred_element_type=F32) * in_decay[:, gs]
        new_st = lax.dot_general(b_g, x_end_b[:, gs], (((0,), (0,)), ((), ())),
                                 preferred_element_type=F32)
        state_ref[:, gs] = chunk_decay[:, gs] * st + new_st

        y = y_diag + y_off + xs[:, gs] * dskip_ref[:, gs]
        y = y * _silu(z_ref[:, gs].astype(F32))
        ssq = ssq + jnp.sum(y * y, axis=-1, keepdims=True)
        y_ref[:, gs] = y

    inv = lax.rsqrt(ssq * (1.0 / D_INNER) + NORM_EPS)
    o_ref[...] = (y_ref[...] * inv * nw_ref[...]).astype(o_ref.dtype)


def _ssd_mixer(proj, dt_raw, conv_w, conv_b, dt_bias, a_log, d_skip, norm_w, *, batch, seq):
    nc = seq // CHUNK
    t = batch * seq
    lanes = jnp.arange(DT_LANES)
    valid = lanes < DT_COPIES * SSD_HEADS
    head_of_lane = lanes % SSD_HEADS

    def tile_heads(v):
        return jnp.where(valid, v[head_of_lane], 0.0).reshape(1, DT_LANES).astype(F32)

    def expand(width):
        cols = jnp.arange(SSD_HEADS * width) // width
        return ((head_of_lane[:, None] == cols[None, :]) & valid[:, None]).astype(BF16)

    e64 = expand(SSD_HEAD_DIM)
    e128 = expand(CHUNK)
    d_skip_e = jnp.repeat(d_skip, SSD_HEAD_DIM).reshape(1, D_INNER)

    tok = lambda b, c: b * nc + c
    const = lambda b, c: (0, 0)
    full = lambda a: pl.BlockSpec(a.shape, const)
    cwx, cwbc = conv_w[:, :D_INNER], conv_w[:, D_INNER:]
    cbx, cbbc = conv_b[:D_INNER].reshape(1, -1), conv_b[D_INNER:].reshape(1, -1)
    dtb, alog = tile_heads(dt_bias), tile_heads(a_log)
    nw = norm_w.reshape(1, D_INNER)
    params = [cwx, cbx, cwbc, cbbc, dtb, alog, d_skip_e, nw, e64, e128]
    return pl.pallas_call(
        _ssd_kernel,
        grid=(batch, nc),
        in_specs=[
            pl.BlockSpec((CHUNK, D_INNER), lambda b, c: (tok(b, c), 0)),
            pl.BlockSpec((CHUNK, D_INNER), lambda b, c: (tok(b, c), 4)),
            pl.BlockSpec((CHUNK, BC_DIM), lambda b, c: (tok(b, c), 10)),
            pl.BlockSpec((CHUNK, DT_LANES), lambda b, c: (tok(b, c), 0)),
        ] + [full(p) for p in params],
        out_specs=pl.BlockSpec((CHUNK, D_INNER), lambda b, c: (tok(b, c), 0)),
        out_shape=jax.ShapeDtypeStruct((t, D_INNER), BF16),
        scratch_shapes=[
            pltpu.VMEM((SSD_STATE, D_INNER), F32),
            pltpu.VMEM((8, D_INNER), F32),
            pltpu.VMEM((8, BC_DIM), F32),
            pltpu.VMEM((8 + CHUNK, D_INNER), F32),
            pltpu.VMEM((8 + CHUNK, BC_DIM), F32),
            pltpu.VMEM((CHUNK, D_INNER), F32),
        ],
        compiler_params=_params(("parallel", "arbitrary")),
        name="ssd_mixer",
    )(proj, proj, proj, dt_raw, *params)


def _sgu_kernel(g_ref, u_ref, v_ref, lnw_ref, lnb_ref, ws_ref, bias_ref, o_ref, wtri_ref):
    @pl.when(pl.program_id(0) == 0)
    def _():
        row = lax.broadcasted_iota(jnp.int32, (CHUNK, CHUNK), 0)
        col = lax.broadcasted_iota(jnp.int32, (CHUNK, CHUNK), 1)
        for g in range(SGU_GROUPS):
            wtri_ref[g] = jnp.where(row >= col, ws_ref[g], 0.0).astype(BF16)

    v = _gelu_tanh(v_ref[...].astype(F32))
    mu = jnp.mean(v, axis=-1, keepdims=True)
    vc = v - mu
    var = jnp.mean(vc * vc, axis=-1, keepdims=True)
    vn = (vc * lax.rsqrt(var + NORM_EPS) * lnw_ref[...] + lnb_ref[...]).astype(BF16)
    for g in range(SGU_GROUPS):
        gs = slice(g * SGU_GROUP_DIM, (g + 1) * SGU_GROUP_DIM)
        mixed = jnp.dot(wtri_ref[g], vn[:, gs], preferred_element_type=F32) + bias_ref[:, gs]
        u = _gelu_tanh(u_ref[:, gs].astype(F32))
        o_ref[:, gs] = (u * mixed * _silu(g_ref[:, gs].astype(F32))).astype(o_ref.dtype)


def _sgu_mixer(proj, ln_w, ln_b, w_s, b_s):
    t = proj.shape[0]
    bias = jnp.repeat(b_s.T, SGU_GROUP_DIM, axis=1)
    const2 = lambda i: (0, 0)
    return pl.pallas_call(
        _sgu_kernel,
        grid=(t // CHUNK,),
        in_specs=[
            pl.BlockSpec((CHUNK, D_INNER), lambda i: (i, 1)),
            pl.BlockSpec((CHUNK, D_INNER), lambda i: (i, 2)),
            pl.BlockSpec((CHUNK, D_INNER), lambda i: (i, 3)),
            pl.BlockSpec((1, D_INNER), const2),
            pl.BlockSpec((1, D_INNER), const2),
            pl.BlockSpec(w_s.shape, lambda i: (0, 0, 0)),
            pl.BlockSpec((CHUNK, D_INNER), const2),
        ],
        out_specs=pl.BlockSpec((CHUNK, D_INNER), lambda i: (i, 0)),
        out_shape=jax.ShapeDtypeStruct((t, D_INNER), BF16),
        scratch_shapes=[pltpu.VMEM((SGU_GROUPS, CHUNK, CHUNK), BF16)],
        compiler_params=_params(("arbitrary",)),
        name="sgu_mixer",
    )(proj, proj, proj, ln_w.reshape(1, -1), ln_b.reshape(1, -1), w_s, bias)


def _attn_kernel(q_ref, k_ref, v_ref, g_ref, o_ref, acc_ref, carry_ref):
    i = pl.program_id(2)
    tq, tk, band, d = ATTN_TQ, ATTN_TK, ATTN_BAND, SB_HEAD_DIM
    srow = lax.broadcasted_iota(jnp.int32, (tk, 2 * tk), 0)
    scol = lax.broadcasted_iota(jnp.int32, (tk, 2 * tk), 1)
    later2 = jnp.where((scol & (tk - 1)) > srow, 1.0, 0.0).astype(BF16)

    def softplus2(z):
        return jnp.maximum(z, 0.0) + jnp.log2(1.0 + jnp.exp2(-jnp.abs(z)))

    def later_sums(sp, nsub):
        hi = sp.astype(BF16)
        lo = (sp - hi.astype(F32)).astype(BF16)
        afts, tots = [], []
        for r in range(nsub):
            rs = slice(r * tk, (r + 1) * tk)
            aft = jnp.dot(later2, jnp.concatenate([hi[rs], lo[rs]], axis=0),
                          preferred_element_type=F32)
            afts.append(aft)
            tots.append(aft[0:1, :] + sp[r * tk:r * tk + 1, :])
        return afts, tots

    def block_weights(z, sp, afts, tots, carry, nsub):
        ws = [None] * nsub
        for r in reversed(range(nsub)):
            rs = slice(r * tk, (r + 1) * tk)
            ws[r] = jnp.exp2((z[rs] - sp[rs]) - afts[r] - carry).astype(BF16)
            carry = carry + tots[r]
        return jnp.concatenate(ws, axis=0), carry

    def values(ks, nkeys, w):
        return lax.dot_general(v_ref[pl.ds(ks, nkeys), :], w, (((0,), (0,)), ((), ())),
                               preferred_element_type=F32)

    tiles = range(ATTN_GROUP)
    nsub = band // tk
    tile_idx = [i * ATTN_GROUP + g for g in tiles]
    starts = [pl.multiple_of(jnp.maximum(t - 1, 0) * tq, tq) for t in tile_idx]
    zs = [lax.dot_general(k_ref[pl.ds(starts[g], band), :], q_ref[g * tq:(g + 1) * tq, :],
                          (((1,), (1,)), ((), ())), preferred_element_type=F32)
          for g in tiles]
    key_off = lax.broadcasted_iota(jnp.int32, (band, tq), 0)
    qry_off = lax.broadcasted_iota(jnp.int32, (band, tq), 1)
    masks = [(starts[g] + key_off) < (tile_idx[g] * tq + qry_off) for g in tiles]
    sps = [jnp.where(masks[g], softplus2(zs[g]), 0.0) for g in tiles]
    sums = [later_sums(sps[g], nsub) for g in tiles]
    for g in tiles:
        w, carry = block_weights(zs[g], sps[g], sums[g][0], sums[g][1],
                                 jnp.zeros((1, tq), F32), nsub)
        w = jnp.where(masks[g], w, jnp.zeros_like(w))
        acc_ref[g] = values(starts[g], band, w)
        carry_ref[g] = carry

    for g in tiles:
        def more(state):
            j, least = state
            return jnp.logical_and(j >= 0, least < ATTN_SKIP_LOG2)

        def earlier(state, g=g):
            j, _ = state
            ks = pl.multiple_of(j * tq, tq)
            z = lax.dot_general(k_ref[pl.ds(ks, tq), :], q_ref[g * tq:(g + 1) * tq, :],
                                (((1,), (1,)), ((), ())), preferred_element_type=F32)
            sp = softplus2(z)
            afts, tots = later_sums(sp, tq // tk)
            w, carry = block_weights(z, sp, afts, tots, carry_ref[g], tq // tk)
            acc_ref[g] += values(ks, tq, w)
            carry_ref[g] = carry
            return j - 1, jnp.min(carry)

        lax.while_loop(more, earlier, (starts[g] // tq - 1, jnp.min(carry_ref[g])))

    for g in tiles:
        rows = slice(g * tq, (g + 1) * tq)
        o_ref[rows, :] = (acc_ref[g].T * _silu(g_ref[rows, :].astype(F32))).astype(o_ref.dtype)


def _attention(qkvg, *, batch, seq):
    rows = ATTN_GROUP * ATTN_TQ
    assert seq % rows == 0 and seq >= ATTN_BAND and ATTN_BAND == 2 * ATTN_TQ
    nq = seq // rows
    t = batch * seq
    return pl.pallas_call(
        _attn_kernel,
        grid=(batch, SB_HEADS, nq),
        in_specs=[
            pl.BlockSpec((rows, SB_HEAD_DIM), lambda b, h, i: (b * nq + i, h)),
            pl.BlockSpec((seq, SB_HEAD_DIM), lambda b, h, i: (b, SB_HEADS + h)),
            pl.BlockSpec((seq, SB_HEAD_DIM), lambda b, h, i: (b, 2 * SB_HEADS + h)),
            pl.BlockSpec((rows, SB_HEAD_DIM), lambda b, h, i: (b * nq + i, 3 * SB_HEADS + h)),
        ],
        out_specs=pl.BlockSpec((rows, SB_HEAD_DIM), lambda b, h, i: (b * nq + i, h)),
        out_shape=jax.ShapeDtypeStruct((t, SB_HEADS * SB_HEAD_DIM), BF16),
        scratch_shapes=[
            pltpu.VMEM((ATTN_GROUP, SB_HEAD_DIM, ATTN_TQ), F32),
            pltpu.VMEM((ATTN_GROUP, 1, ATTN_TQ), F32),
        ],
        compiler_params=_params(("parallel", "parallel", "arbitrary")),
        name="stickbreak_attn",
    )(qkvg, qkvg, qkvg, qkvg)


EVEN_TN = 512
EVEN_ALIGNED = 2 * D_INNER + BC_DIM
EVEN_AFTER_DT = EVEN_ALIGNED + SSD_HEADS


def _even_in_weights(ev_w_in):
    w_guv = ev_w_in[:, :, EVEN_AFTER_DT:].astype(BF16)
    w_dt = ev_w_in[:, :, EVEN_ALIGNED:EVEN_AFTER_DT]
    pad = jnp.zeros(w_dt.shape[:2] + (DT_LANES - DT_COPIES * SSD_HEADS,), F32)
    w_dt = jnp.concatenate([w_dt] * DT_COPIES + [pad], axis=2).astype(BF16)
    return w_guv, w_dt


def _even_layer(h, norm_w, ev_w_in, w_guv, w_dt, ev_w_out, layer, conv_w, conv_b, dt_bias, a_log,
                d_skip, ssd_norm_w, sgu_ln_w, sgu_ln_b, sgu_w, sgu_b, *, batch, seq, in_place):
    tiles = lambda cols: cols // EVEN_TN
    proj, dt_raw = _in_proj(
        h, norm_w,
        [(ev_w_in, layer, tiles(EVEN_ALIGNED)), (w_guv, layer, tiles(3 * D_INNER))],
        tm=1024, tn=EVEN_TN,
        out_tiles=[(0, tiles(D_INNER), 0),
                   (tiles(D_INNER), tiles(D_INNER + BC_DIM), tiles(4 * D_INNER)),
                   (tiles(EVEN_ALIGNED), tiles(3 * D_INNER), tiles(D_INNER))],
        w_dt=w_dt[layer])
    y_a = _ssd_mixer(proj, dt_raw, conv_w, conv_b, dt_bias, a_log, d_skip, ssd_norm_w,
                     batch=batch, seq=seq)
    y_b = _sgu_mixer(proj, sgu_ln_w, sgu_ln_b, sgu_w, sgu_b)
    return _out_proj([y_a, y_b], [(ev_w_out, layer, 0), (ev_w_out, layer, 1)], h, None,
                     tm=1024, tn=512, in_place=in_place)


def _odd_layer(h, norm_w, od_w_in, od_w_out, layer, final_w, *, batch, seq):
    scale = SB_HEAD_DIM ** -0.5 * math.log2(math.e)
    tn = 1024
    qkvg = _in_proj(h, norm_w, [(od_w_in, layer, 4 * D_INNER // tn)], tm=1024, tn=tn,
                    scaled_tiles=D_INNER // tn, scale=scale)
    y = _attention(qkvg, batch=batch, seq=seq)
    if final_w is None:
        return _out_proj([y], [(od_w_out, layer, 0)], h, None, tm=1024, tn=512, in_place=True)
    w_last = od_w_out[layer:layer + 1].astype(BF16)
    return _out_proj([y], [(w_last, 0, 0)], h, final_w, tm=512, tn=D_MODEL, in_place=True)


def kernel(x, norm_w, final_norm_w, ev_w_in, ev_conv_w, ev_conv_b, ev_dt_bias, ev_a_log,
           ev_d_skip, ev_ssd_norm_w, ev_sgu_ln_w, ev_sgu_ln_b, ev_sgu_w, ev_sgu_b, ev_w_out,
           od_w_in, od_w_out):
    batch, seq, d = x.shape
    depth = norm_w.shape[0]
    h = x.reshape(batch * seq, d)
    w_guv, w_dt = _even_in_weights(ev_w_in)
    for layer in range(depth):
        i = layer // 2
        if layer % 2 == 0:
            h = _even_layer(h, norm_w[layer], ev_w_in, w_guv, w_dt, ev_w_out, i, ev_conv_w[i],
                            ev_conv_b[i], ev_dt_bias[i], ev_a_log[i], ev_d_skip[i],
                            ev_ssd_norm_w[i], ev_sgu_ln_w[i], ev_sgu_ln_b[i], ev_sgu_w[i],
                            ev_sgu_b[i], batch=batch, seq=seq, in_place=layer > 0)
        else:
            fw = final_norm_w if layer == depth - 1 else None
            h = _odd_layer(h, norm_w[layer], od_w_in, od_w_out, i, fw, batch=batch, seq=seq)
    if depth % 2 == 1:
        raise NotImplementedError("final RMSNorm is fused into an odd last layer")
    return h.reshape(batch, seq, d)
```

```python
import functools
import math

import jax
import jax.numpy as jnp
from jax import lax
from jax.experimental import pallas as pl
from jax.experimental.pallas import tpu as pltpu

F32 = jnp.float32
BF16 = jnp.bfloat16

D_MODEL = 2048
NORM_EPS = 1e-5
SSD_HEAD_DIM = 64
SSD_HEADS = 32
SSD_GROUPS = 4
SSD_STATE = 128
CHUNK = 128
CONV_WIDTH = 4
D_INNER = 2048
BC_DIM = 2 * SSD_GROUPS * SSD_STATE
GROUP_COLS = D_INNER // SSD_GROUPS
SGU_GROUPS = 16
SGU_GROUP_DIM = 128
SB_HEADS = 16
SB_HEAD_DIM = 128
ATTN_TQ = 256
ATTN_TK = 128
ATTN_BAND = 2 * ATTN_TQ
ATTN_GROUP = 4
ATTN_SKIP_LOG2 = 160.0
DT_LANES = 128
DT_COPIES = 3

V7X_VMEM_LIMIT = 48 * 1024 * 1024
V7X_F32_SUBLANES = 8


def _params(sem):
    return pltpu.CompilerParams(dimension_semantics=sem, vmem_limit_bytes=V7X_VMEM_LIMIT)


def _sigmoid(x):
    return 1.0 / (1.0 + jnp.exp(-x))


def _silu(x):
    return x * _sigmoid(x)


def _gelu_tanh(x):
    c = math.sqrt(2.0 / math.pi)
    return x * (0.5 * (1.0 + jnp.tanh(c * (x + 0.044715 * (x * x * x)))))


def _split3(x):
    h1 = x.astype(BF16)
    r1 = x - h1.astype(F32)
    h2 = r1.astype(BF16)
    r2 = r1 - h2.astype(F32)
    return h1, h2, r2.astype(BF16)


def _in_proj_kernel(*refs, transposed, scaled_tiles, scale, with_dt):
    x_ref, nw_ref, w_ref = refs[:3]
    hn_ref = refs[-1]
    if with_dt:
        wdt_ref, o_ref, dt_ref = refs[3:6]
    else:
        o_ref = refs[3]
    j = pl.program_id(1)

    def matmul(w_ref):
        w = w_ref[...].astype(BF16)
        if transposed:
            return lax.dot_general(hn_ref[...], w, (((1,), (1,)), ((), ())),
                                   preferred_element_type=F32)
        return jnp.dot(hn_ref[...], w, preferred_element_type=F32)

    @pl.when(j == 0)
    def _():
        x = x_ref[...]
        ms = jnp.mean(x * x, axis=-1, keepdims=True)
        hn_ref[...] = (x * lax.rsqrt(ms + NORM_EPS) * nw_ref[...]).astype(BF16)
        if with_dt:
            dt_ref[...] = matmul(wdt_ref)

    res = matmul(w_ref)
    if scaled_tiles:
        res = res * jnp.where(j < scaled_tiles, scale, 1.0)
    o_ref[...] = res.astype(o_ref.dtype)


def _in_proj(h, norm_w, w, layer, *, tm, tn, segments, transposed=False, scaled_tiles=0,
             scale=1.0, dt_start=None):
    m, d = h.shape
    n_total = sum(cnt for cnt, _, _ in segments)

    def pick(j, field, unit=1):
        out, first = 0, 0
        for seg in segments:
            inside = jnp.logical_and(j >= first, j < first + seg[0])
            step = tn // unit if field == 1 else 1
            out = out + jnp.where(inside, (j - first) * step + seg[field] // (unit if field == 1 else 1), 0)
            first += seg[0]
        return out

    if transposed:
        n_cols = w.shape[1]
        w = w.reshape(-1, d)
        sub = V7X_F32_SUBLANES
        assert n_cols % sub == 0 and all(first % sub == 0 for _, first, _ in segments)
        w_spec = pl.BlockSpec((pl.Element(tn), pl.Element(d)),
                              lambda i, j: (sub * (layer * n_cols // sub + pick(j, 1, sub)), 0))
    else:
        assert all(first % tn == 0 for _, first, _ in segments)
        w_spec = pl.BlockSpec((None, d, tn), lambda i, j: (layer, 0, pick(j, 1, tn)))
    in_specs = [pl.BlockSpec((tm, d), lambda i, j: (i, 0)),
                pl.BlockSpec((1, d), lambda i, j: (0, 0)),
                w_spec]
    args = [h, norm_w.reshape(1, d), w]
    out_specs = pl.BlockSpec((tm, tn), lambda i, j: (i, pick(j, 2)))
    out_shape = jax.ShapeDtypeStruct((m, n_total * tn), BF16)
    if dt_start is not None:
        assert transposed
        in_specs.append(pl.BlockSpec((pl.Element(DT_LANES), pl.Element(d)),
                                     lambda i, j: (sub * ((layer * n_cols + dt_start) // sub), 0)))
        args.append(w)
        out_specs = [out_specs, pl.BlockSpec((tm, DT_LANES), lambda i, j: (i, 0))]
        out_shape = [out_shape, jax.ShapeDtypeStruct((m, DT_LANES), F32)]
    return pl.pallas_call(
        functools.partial(_in_proj_kernel, transposed=transposed, scaled_tiles=scaled_tiles,
                          scale=scale, with_dt=dt_start is not None),
        grid=(m // tm, n_total),
        in_specs=in_specs,
        out_specs=out_specs,
        out_shape=out_shape,
        scratch_shapes=[pltpu.VMEM((tm, d), BF16)],
        compiler_params=_params(("parallel", "arbitrary")),
        name="norm_in_proj",
    )(*args)


def _out_proj_kernel(*refs, n_in, final_norm):
    y_refs = refs[:n_in]
    w_refs = refs[n_in:2 * n_in]
    h_ref = refs[2 * n_in]
    o_ref = refs[-1]
    acc = h_ref[...]
    for y_ref, w_ref in zip(y_refs, w_refs):
        acc = acc + jnp.dot(y_ref[...], w_ref[...].astype(BF16), preferred_element_type=F32)
    if final_norm:
        fw_ref = refs[2 * n_in + 1]
        ms = jnp.mean(acc * acc, axis=-1, keepdims=True)
        acc = acc * lax.rsqrt(ms + NORM_EPS) * fw_ref[...]
    o_ref[...] = acc


def _out_proj(ys, ws, h, final_w, *, tm, tn, in_place):
    m, n = h.shape
    n_in = len(ys)
    final_norm = final_w is not None
    if final_norm:
        assert tn == n

    def w_spec(kdim, layer, row_block):
        return pl.BlockSpec((None, kdim, tn), lambda i, j: (layer, row_block, j))

    in_specs = [pl.BlockSpec((tm, y.shape[1]), lambda i, j: (i, 0)) for y in ys]
    in_specs += [w_spec(y.shape[1], layer, rb) for y, (_, layer, rb) in zip(ys, ws)]
    in_specs += [pl.BlockSpec((tm, tn), lambda i, j: (i, j))]
    args = list(ys) + [w for w, _, _ in ws] + [h]
    if final_norm:
        in_specs += [pl.BlockSpec((1, n), lambda i, j: (0, 0))]
        args += [final_w.reshape(1, n)]
    return pl.pallas_call(
        functools.partial(_out_proj_kernel, n_in=n_in, final_norm=final_norm),
        grid=(m // tm, n // tn),
        in_specs=in_specs,
        out_specs=pl.BlockSpec((tm, tn), lambda i, j: (i, j)),
        out_shape=jax.ShapeDtypeStruct((m, n), F32),
        input_output_aliases={2 * n_in: 0} if in_place else {},
        compiler_params=_params(("parallel", "parallel")),
        name="out_proj",
    )(*args)


def _causal_conv(ext_ref, tail_ref, cur, w_ref, b_ref):
    ext_ref[0:8, :] = tail_ref[...]
    ext_ref[8:8 + CHUNK, :] = cur
    tail_ref[...] = cur[CHUNK - 8:, :]
    acc = b_ref[...] + w_ref[CONV_WIDTH - 1:CONV_WIDTH, :] * cur
    for k in range(CONV_WIDTH - 1):
        d = CONV_WIDTH - 1 - k
        acc = acc + w_ref[k:k + 1, :] * ext_ref[8 - d:8 - d + CHUNK, :]
    return acc


def _ssd_kernel(z_ref, x_ref, bc_ref, dt_ref, cwx_ref, cbx_ref, cwbc_ref, cbbc_ref,
                dtb_ref, alog_ref, dskip_ref, nw_ref, e64_ref, e128_ref,
                o_ref,
                state_ref, xtail_ref, bctail_ref, xext_ref, bcext_ref, y_ref):
    c = pl.program_id(1)

    @pl.when(c == 0)
    def _():
        state_ref[...] = jnp.zeros_like(state_ref)
        xtail_ref[...] = jnp.zeros_like(xtail_ref)
        bctail_ref[...] = jnp.zeros_like(bctail_ref)

    L = CHUNK
    row = lax.broadcasted_iota(jnp.int32, (L, L), 0)
    col = lax.broadcasted_iota(jnp.int32, (L, L), 1)
    causal = row >= col
    tri = jnp.where(causal, 1.0, 0.0).astype(BF16)
    head_lane = lax.broadcasted_iota(jnp.int32, (L, DT_LANES), 1) < SSD_HEADS

    dt_in = dt_ref[...] + dtb_ref[...]
    dt = jnp.where(head_lane, jnp.maximum(dt_in, 0.0) + jnp.log1p(jnp.exp(-jnp.abs(dt_in))), 0.0)
    a_dt = dt * (-jnp.exp(alog_ref[...]))
    p1, p2, p3 = _split3(a_dt)
    cs = (jnp.dot(tri, p1, preferred_element_type=F32)
          + jnp.dot(tri, p2, preferred_element_type=F32)
          + jnp.dot(tri, p3, preferred_element_type=F32))
    cs_t = cs.T

    def lane_pieces(v):
        q1, q2, q3 = (q.astype(F32) for q in _split3(v))
        return (q1 + pltpu.roll(q2, SSD_HEADS, 1) + pltpu.roll(q3, 2 * SSD_HEADS, 1)).astype(BF16)

    dt_p = lane_pieces(dt)
    cs_p = lane_pieces(cs)
    e64 = e64_ref[...]
    dt_e = jnp.dot(dt_p, e64, preferred_element_type=F32)
    cs_e = jnp.dot(cs_p, e64, preferred_element_type=F32)
    cs_col = jnp.dot(cs_p, e128_ref[...], preferred_element_type=F32)

    xs = _silu(_causal_conv(xext_ref, xtail_ref, x_ref[...].astype(F32), cwx_ref, cbx_ref))
    bc = _silu(_causal_conv(bcext_ref, bctail_ref, bc_ref[...].astype(F32), cwbc_ref, cbbc_ref))

    cs_last = cs_e[L - 1:L, :]
    x_dt = xs * dt_e
    x_dt_b = x_dt.astype(BF16)
    x_end_b = (x_dt * jnp.exp(cs_last - cs_e)).astype(BF16)
    in_decay = jnp.exp(cs_e)
    chunk_decay = jnp.exp(cs_last)

    lane_l = lax.broadcasted_iota(jnp.int32, (L, 2 * SSD_HEAD_DIM), 1)
    first_head = lane_l < SSD_HEAD_DIM
    ssq = jnp.zeros((L, 1), F32)
    for g in range(SSD_GROUPS):
        gs = slice(g * GROUP_COLS, (g + 1) * GROUP_COLS)
        b_g = bc[:, g * SSD_STATE:(g + 1) * SSD_STATE].astype(BF16)
        c_g = bc[:, (SSD_GROUPS + g) * SSD_STATE:(SSD_GROUPS + g + 1) * SSD_STATE].astype(BF16)
        scores = lax.dot_general(c_g, b_g, (((1,), (1,)), ((), ())),
                                 preferred_element_type=F32)
        heads_per_group = SSD_HEADS // SSD_GROUPS
        y_parts = []
        for pair in range(heads_per_group // 2):
            ms = []
            for r in range(2):
                h = g * heads_per_group + 2 * pair + r
                seg = cs_col[:, h * L:(h + 1) * L] - cs_t[h:h + 1, :]
                decay = jnp.exp(jnp.where(causal, seg, -jnp.inf))
                ms.append((scores * decay).astype(BF16))
            lhs = jnp.concatenate(ms, axis=1)
            c0 = g * GROUP_COLS + pair * 2 * SSD_HEAD_DIM
            xp = x_dt_b[:, c0:c0 + 2 * SSD_HEAD_DIM]
            zero = jnp.zeros_like(xp)
            rhs = jnp.concatenate([jnp.where(first_head, xp, zero),
                                   jnp.where(first_head, zero, xp)], axis=0)
            y_parts.append(jnp.dot(lhs, rhs, preferred_element_type=F32))
        y_diag = jnp.concatenate(y_parts, axis=1)

        st = state_ref[:, gs]
        y_off = jnp.dot(c_g, st.astype(BF16), preferred_element_type=F32) * in_decay[:, gs]
        new_st = lax.dot_general(b_g, x_end_b[:, gs], (((0,), (0,)), ((), ())),
                                 preferred_element_type=F32)
        state_ref[:, gs] = chunk_decay[:, gs] * st + new_st

        y = y_diag + y_off + xs[:, gs] * dskip_ref[:, gs]
        y = y * _silu(z_ref[:, gs].astype(F32))
        ssq = ssq + jnp.sum(y * y, axis=-1, keepdims=True)
        y_ref[:, gs] = y

    inv = lax.rsqrt(ssq * (1.0 / D_INNER) + NORM_EPS)
    o_ref[...] = (y_ref[...] * inv * nw_ref[...]).astype(o_ref.dtype)


def _ssd_mixer(proj, dt_raw, conv_w, conv_b, dt_bias, a_log, d_skip, norm_w, *, batch, seq):
    nc = seq // CHUNK
    t = batch * seq
    lanes = jnp.arange(DT_LANES)
    valid = lanes < DT_COPIES * SSD_HEADS
    head_of_lane = lanes % SSD_HEADS

    def tile_heads(v):
        return jnp.pad(v.astype(F32), (0, DT_LANES - SSD_HEADS)).reshape(1, DT_LANES)

    def expand(width):
        cols = jnp.arange(SSD_HEADS * width) // width
        return ((head_of_lane[:, None] == cols[None, :]) & valid[:, None]).astype(BF16)

    e64 = expand(SSD_HEAD_DIM)
    e128 = expand(CHUNK)
    d_skip_e = jnp.repeat(d_skip, SSD_HEAD_DIM).reshape(1, D_INNER)

    tok = lambda b, c: b * nc + c
    const = lambda b, c: (0, 0)
    full = lambda a: pl.BlockSpec(a.shape, const)
    cwx, cwbc = conv_w[:, :D_INNER], conv_w[:, D_INNER:]
    cbx, cbbc = conv_b[:D_INNER].reshape(1, -1), conv_b[D_INNER:].reshape(1, -1)
    dtb, alog = tile_heads(dt_bias), tile_heads(a_log)
    nw = norm_w.reshape(1, D_INNER)
    params = [cwx, cbx, cwbc, cbbc, dtb, alog, d_skip_e, nw, e64, e128]
    return pl.pallas_call(
        _ssd_kernel,
        grid=(batch, nc),
        in_specs=[
            pl.BlockSpec((CHUNK, D_INNER), lambda b, c: (tok(b, c), 0)),
            pl.BlockSpec((CHUNK, D_INNER), lambda b, c: (tok(b, c), 4)),
            pl.BlockSpec((CHUNK, BC_DIM), lambda b, c: (tok(b, c), 10)),
            pl.BlockSpec((CHUNK, DT_LANES), lambda b, c: (tok(b, c), 0)),
        ] + [full(p) for p in params],
        out_specs=pl.BlockSpec((CHUNK, D_INNER), lambda b, c: (tok(b, c), 0)),
        out_shape=jax.ShapeDtypeStruct((t, D_INNER), BF16),
        scratch_shapes=[
            pltpu.VMEM((SSD_STATE, D_INNER), F32),
            pltpu.VMEM((8, D_INNER), F32),
            pltpu.VMEM((8, BC_DIM), F32),
            pltpu.VMEM((8 + CHUNK, D_INNER), F32),
            pltpu.VMEM((8 + CHUNK, BC_DIM), F32),
            pltpu.VMEM((CHUNK, D_INNER), F32),
        ],
        compiler_params=_params(("parallel", "arbitrary")),
        name="ssd_mixer",
    )(proj, proj, proj, dt_raw, *params)


def _sgu_kernel(g_ref, u_ref, v_ref, lnw_ref, lnb_ref, ws_ref, bias_ref, o_ref, wtri_ref):
    @pl.when(pl.program_id(0) == 0)
    def _():
        row = lax.broadcasted_iota(jnp.int32, (CHUNK, CHUNK), 0)
        col = lax.broadcasted_iota(jnp.int32, (CHUNK, CHUNK), 1)
        for g in range(SGU_GROUPS):
            wtri_ref[g] = jnp.where(row >= col, ws_ref[g], 0.0).astype(BF16)

    v = _gelu_tanh(v_ref[...].astype(F32))
    mu = jnp.mean(v, axis=-1, keepdims=True)
    vc = v - mu
    var = jnp.mean(vc * vc, axis=-1, keepdims=True)
    vn = (vc * lax.rsqrt(var + NORM_EPS) * lnw_ref[...] + lnb_ref[...]).astype(BF16)
    for g in range(SGU_GROUPS):
        gs = slice(g * SGU_GROUP_DIM, (g + 1) * SGU_GROUP_DIM)
        mixed = jnp.dot(wtri_ref[g], vn[:, gs], preferred_element_type=F32) + bias_ref[:, gs]
        u = _gelu_tanh(u_ref[:, gs].astype(F32))
        o_ref[:, gs] = (u * mixed * _silu(g_ref[:, gs].astype(F32))).astype(o_ref.dtype)


def _sgu_mixer(proj, ln_w, ln_b, w_s, b_s):
    t = proj.shape[0]
    bias = jnp.repeat(b_s.T, SGU_GROUP_DIM, axis=1)
    const2 = lambda i: (0, 0)
    return pl.pallas_call(
        _sgu_kernel,
        grid=(t // CHUNK,),
        in_specs=[
            pl.BlockSpec((CHUNK, D_INNER), lambda i: (i, 1)),
            pl.BlockSpec((CHUNK, D_INNER), lambda i: (i, 2)),
            pl.BlockSpec((CHUNK, D_INNER), lambda i: (i, 3)),
            pl.BlockSpec((1, D_INNER), const2),
            pl.BlockSpec((1, D_INNER), const2),
            pl.BlockSpec(w_s.shape, lambda i: (0, 0, 0)),
            pl.BlockSpec((CHUNK, D_INNER), const2),
        ],
        out_specs=pl.BlockSpec((CHUNK, D_INNER), lambda i: (i, 0)),
        out_shape=jax.ShapeDtypeStruct((t, D_INNER), BF16),
        scratch_shapes=[pltpu.VMEM((SGU_GROUPS, CHUNK, CHUNK), BF16)],
        compiler_params=_params(("arbitrary",)),
        name="sgu_mixer",
    )(proj, proj, proj, ln_w.reshape(1, -1), ln_b.reshape(1, -1), w_s, bias)


def _attn_kernel(q_ref, k_ref, v_ref, g_ref, o_ref, acc_ref, carry_ref):
    i = pl.program_id(2)
    tq, tk, band, d = ATTN_TQ, ATTN_TK, ATTN_BAND, SB_HEAD_DIM
    srow = lax.broadcasted_iota(jnp.int32, (tk, 2 * tk), 0)
    scol = lax.broadcasted_iota(jnp.int32, (tk, 2 * tk), 1)
    later2 = jnp.where((scol & (tk - 1)) > srow, 1.0, 0.0).astype(BF16)

    def softplus2(z):
        return jnp.maximum(z, 0.0) + jnp.log2(1.0 + jnp.exp2(-jnp.abs(z)))

    def later_sums(sp, nsub):
        hi = sp.astype(BF16)
        lo = (sp - hi.astype(F32)).astype(BF16)
        afts, tots = [], []
        for r in range(nsub):
            rs = slice(r * tk, (r + 1) * tk)
            aft = jnp.dot(later2, jnp.concatenate([hi[rs], lo[rs]], axis=0),
                          preferred_element_type=F32)
            afts.append(aft)
            tots.append(aft[0:1, :] + sp[r * tk:r * tk + 1, :])
        return afts, tots

    def block_weights(z, sp, afts, tots, carry, nsub):
        ws = [None] * nsub
        for r in reversed(range(nsub)):
            rs = slice(r * tk, (r + 1) * tk)
            ws[r] = jnp.exp2((z[rs] - sp[rs]) - afts[r] - carry).astype(BF16)
            carry = carry + tots[r]
        return jnp.concatenate(ws, axis=0), carry

    def values(ks, nkeys, w):
        return lax.dot_general(v_ref[pl.ds(ks, nkeys), :], w, (((0,), (0,)), ((), ())),
                               preferred_element_type=F32)

    tiles = range(ATTN_GROUP)
    nsub = band // tk
    tile_idx = [i * ATTN_GROUP + g for g in tiles]
    starts = [pl.multiple_of(jnp.maximum(t - 1, 0) * tq, tq) for t in tile_idx]
    zs = [lax.dot_general(k_ref[pl.ds(starts[g], band), :], q_ref[g * tq:(g + 1) * tq, :],
                          (((1,), (1,)), ((), ())), preferred_element_type=F32)
          for g in tiles]
    key_off = lax.broadcasted_iota(jnp.int32, (band, tq), 0)
    qry_off = lax.broadcasted_iota(jnp.int32, (band, tq), 1)
    masks = [(starts[g] + key_off) < (tile_idx[g] * tq + qry_off) for g in tiles]
    sps = [jnp.where(masks[g], softplus2(zs[g]), 0.0) for g in tiles]
    sums = [later_sums(sps[g], nsub) for g in tiles]
    for g in tiles:
        w, carry = block_weights(zs[g], sps[g], sums[g][0], sums[g][1],
                                 jnp.zeros((1, tq), F32), nsub)
        w = jnp.where(masks[g], w, jnp.zeros_like(w))
        acc_ref[g] = values(starts[g], band, w)
        carry_ref[g] = carry

    for g in tiles:
        def more(state):
            j, least = state
            return jnp.logical_and(j >= 0, least < ATTN_SKIP_LOG2)

        def earlier(state, g=g):
            j, _ = state
            ks = pl.multiple_of(j * tq, tq)
            z = lax.dot_general(k_ref[pl.ds(ks, tq), :], q_ref[g * tq:(g + 1) * tq, :],
                                (((1,), (1,)), ((), ())), preferred_element_type=F32)
            sp = softplus2(z)
            afts, tots = later_sums(sp, tq // tk)
            w, carry = block_weights(z, sp, afts, tots, carry_ref[g], tq // tk)
            acc_ref[g] += values(ks, tq, w)
            carry_ref[g] = carry
            return j - 1, jnp.min(carry)

        lax.while_loop(more, earlier, (starts[g] // tq - 1, jnp.min(carry_ref[g])))

    for g in tiles:
        rows = slice(g * tq, (g + 1) * tq)
        o_ref[rows, :] = (acc_ref[g].T * _silu(g_ref[rows, :].astype(F32))).astype(o_ref.dtype)


def _attention(qkvg, *, batch, seq):
    rows = ATTN_GROUP * ATTN_TQ
    assert seq % rows == 0 and seq >= ATTN_BAND and ATTN_BAND == 2 * ATTN_TQ
    nq = seq // rows
    t = batch * seq
    return pl.pallas_call(
        _attn_kernel,
        grid=(batch, SB_HEADS, nq),
        in_specs=[
            pl.BlockSpec((rows, SB_HEAD_DIM), lambda b, h, i: (b * nq + i, h)),
            pl.BlockSpec((seq, SB_HEAD_DIM), lambda b, h, i: (b, SB_HEADS + h)),
            pl.BlockSpec((seq, SB_HEAD_DIM), lambda b, h, i: (b, 2 * SB_HEADS + h)),
            pl.BlockSpec((rows, SB_HEAD_DIM), lambda b, h, i: (b * nq + i, 3 * SB_HEADS + h)),
        ],
        out_specs=pl.BlockSpec((rows, SB_HEAD_DIM), lambda b, h, i: (b * nq + i, h)),
        out_shape=jax.ShapeDtypeStruct((t, SB_HEADS * SB_HEAD_DIM), BF16),
        scratch_shapes=[
            pltpu.VMEM((ATTN_GROUP, SB_HEAD_DIM, ATTN_TQ), F32),
            pltpu.VMEM((ATTN_GROUP, 1, ATTN_TQ), F32),
        ],
        compiler_params=_params(("parallel", "parallel", "arbitrary")),
        name="stickbreak_attn",
    )(qkvg, qkvg, qkvg, qkvg)


EVEN_TN = 1024
EVEN_ALIGNED = 2 * D_INNER + BC_DIM
EVEN_AFTER_DT = EVEN_ALIGNED + SSD_HEADS


def _even_layer(h, norm_w, ev_w_in_t, ev_w_out, layer, conv_w, conv_b, dt_bias, a_log,
                d_skip, ssd_norm_w, sgu_ln_w, sgu_ln_b, sgu_w, sgu_b, *, batch, seq, in_place):
    tiles = lambda cols: cols // EVEN_TN
    proj, dt_raw = _in_proj(
        h, norm_w, ev_w_in_t, layer, tm=1024, tn=EVEN_TN, transposed=True,
        segments=[(tiles(D_INNER), 0, 0),
                  (tiles(D_INNER + BC_DIM), D_INNER, tiles(4 * D_INNER)),
                  (tiles(3 * D_INNER), EVEN_AFTER_DT, tiles(D_INNER))],
        dt_start=EVEN_ALIGNED)
    y_a = _ssd_mixer(proj, dt_raw, conv_w, conv_b, dt_bias, a_log, d_skip, ssd_norm_w,
                     batch=batch, seq=seq)
    y_b = _sgu_mixer(proj, sgu_ln_w, sgu_ln_b, sgu_w, sgu_b)
    return _out_proj([y_a, y_b], [(ev_w_out, layer, 0), (ev_w_out, layer, 1)], h, None,
                     tm=1024, tn=512, in_place=in_place)


def _odd_layer(h, norm_w, od_w_in, od_w_out, layer, final_w, *, batch, seq):
    scale = SB_HEAD_DIM ** -0.5 * math.log2(math.e)
    tn = 1024
    qkvg = _in_proj(h, norm_w, od_w_in, layer, tm=1024, tn=tn,
                    segments=[(4 * D_INNER // tn, 0, 0)], scaled_tiles=D_INNER // tn, scale=scale)
    y = _attention(qkvg, batch=batch, seq=seq)
    if final_w is None:
        return _out_proj([y], [(od_w_out, layer, 0)], h, None, tm=1024, tn=512, in_place=True)
    w_last = od_w_out[layer:layer + 1].astype(BF16)
    return _out_proj([y], [(w_last, 0, 0)], h, final_w, tm=512, tn=D_MODEL, in_place=True)


def kernel(x, norm_w, final_norm_w, ev_w_in, ev_conv_w, ev_conv_b, ev_dt_bias, ev_a_log,
           ev_d_skip, ev_ssd_norm_w, ev_sgu_ln_w, ev_sgu_ln_b, ev_sgu_w, ev_sgu_b, ev_w_out,
           od_w_in, od_w_out):
    batch, seq, d = x.shape
    depth = norm_w.shape[0]
    h = x.reshape(batch * seq, d)
    ev_w_in_t = jnp.swapaxes(ev_w_in, 1, 2)
    for layer in range(depth):
        i = layer // 2
        if layer % 2 == 0:
            h = _even_layer(h, norm_w[layer], ev_w_in_t, ev_w_out, i, ev_conv_w[i],
                            ev_conv_b[i], ev_dt_bias[i], ev_a_log[i], ev_d_skip[i],
                            ev_ssd_norm_w[i], ev_sgu_ln_w[i], ev_sgu_ln_b[i], ev_sgu_w[i],
                            ev_sgu_b[i], batch=batch, seq=seq, in_place=layer > 0)
        else:
            fw = final_norm_w if layer == depth - 1 else None
            h = _odd_layer(h, norm_w[layer], od_w_in, od_w_out, i, fw, batch=batch, seq=seq)
    if depth % 2 == 1:
        raise NotImplementedError("final RMSNorm is fused into an odd last layer")
    return h.reshape(batch, seq, d)
```

```python
import functools
import math

import jax
import jax.numpy as jnp
from jax import lax
from jax.experimental import pallas as pl
from jax.experimental.pallas import tpu as pltpu

F32 = jnp.float32
BF16 = jnp.bfloat16

D_MODEL = 2048
NORM_EPS = 1e-5
SSD_HEAD_DIM = 64
SSD_HEADS = 32
SSD_GROUPS = 4
SSD_STATE = 128
CHUNK = 128
CONV_WIDTH = 4
D_INNER = 2048
BC_DIM = 2 * SSD_GROUPS * SSD_STATE
GROUP_COLS = D_INNER // SSD_GROUPS
SGU_GROUPS = 16
SGU_GROUP_DIM = 128
SB_HEADS = 16
SB_HEAD_DIM = 128
ATTN_TQ = 256
ATTN_TK = 128
ATTN_GROUP = 8
ATTN_SKIP_LOG2 = 160.0
DT_LANES = 128
DT_COPIES = 3

V7X_VMEM_LIMIT = 48 * 1024 * 1024
V7X_F32_SUBLANES = 8


def _params(sem):
    return pltpu.CompilerParams(dimension_semantics=sem, vmem_limit_bytes=V7X_VMEM_LIMIT)


def _sigmoid(x):
    return 1.0 / (1.0 + jnp.exp(-x))


def _silu(x):
    return x * _sigmoid(x)


def _gelu_tanh(x):
    c = math.sqrt(2.0 / math.pi)
    return x * (0.5 * (1.0 + jnp.tanh(c * (x + 0.044715 * (x * x * x)))))


def _split3(x):
    h1 = x.astype(BF16)
    r1 = x - h1.astype(F32)
    h2 = r1.astype(BF16)
    r2 = r1 - h2.astype(F32)
    return h1, h2, r2.astype(BF16)


def _in_proj_kernel(*refs, transposed, scaled_tiles, scale, with_dt):
    x_ref, nw_ref, w_ref = refs[:3]
    hn_ref = refs[-1]
    if with_dt:
        wdt_ref, o_ref, dt_ref = refs[3:6]
    else:
        o_ref = refs[3]
    j = pl.program_id(1)

    def matmul(w_ref):
        w = w_ref[...].astype(BF16)
        if transposed:
            return lax.dot_general(hn_ref[...], w, (((1,), (1,)), ((), ())),
                                   preferred_element_type=F32)
        return jnp.dot(hn_ref[...], w, preferred_element_type=F32)

    @pl.when(j == 0)
    def _():
        x = x_ref[...]
        ms = jnp.mean(x * x, axis=-1, keepdims=True)
        hn_ref[...] = (x * lax.rsqrt(ms + NORM_EPS) * nw_ref[...]).astype(BF16)
        if with_dt:
            dt_ref[...] = matmul(wdt_ref)

    res = matmul(w_ref)
    if scaled_tiles:
        res = res * jnp.where(j < scaled_tiles, scale, 1.0)
    o_ref[...] = res.astype(o_ref.dtype)


def _in_proj(h, norm_w, w, layer, *, tm, tn, segments, transposed=False, scaled_tiles=0,
             scale=1.0, dt_start=None):
    m, d = h.shape
    n_total = sum(cnt for cnt, _, _ in segments)

    def pick(j, field, unit=1):
        out, first = 0, 0
        for seg in segments:
            inside = jnp.logical_and(j >= first, j < first + seg[0])
            step = tn // unit if field == 1 else 1
            out = out + jnp.where(inside, (j - first) * step + seg[field] // (unit if field == 1 else 1), 0)
            first += seg[0]
        return out

    if transposed:
        n_cols = w.shape[1]
        w = w.reshape(-1, d)
        sub = V7X_F32_SUBLANES
        assert n_cols % sub == 0 and all(first % sub == 0 for _, first, _ in segments)
        w_spec = pl.BlockSpec((pl.Element(tn), pl.Element(d)),
                              lambda i, j: (sub * (layer * n_cols // sub + pick(j, 1, sub)), 0))
    else:
        assert all(first % tn == 0 for _, first, _ in segments)
        w_spec = pl.BlockSpec((None, d, tn), lambda i, j: (layer, 0, pick(j, 1, tn)))
    in_specs = [pl.BlockSpec((tm, d), lambda i, j: (i, 0)),
                pl.BlockSpec((1, d), lambda i, j: (0, 0)),
                w_spec]
    args = [h, norm_w.reshape(1, d), w]
    out_specs = pl.BlockSpec((tm, tn), lambda i, j: (i, pick(j, 2)))
    out_shape = jax.ShapeDtypeStruct((m, n_total * tn), BF16)
    if dt_start is not None:
        assert transposed
        in_specs.append(pl.BlockSpec((pl.Element(DT_LANES), pl.Element(d)),
                                     lambda i, j: (sub * ((layer * n_cols + dt_start) // sub), 0)))
        args.append(w)
        out_specs = [out_specs, pl.BlockSpec((tm, DT_LANES), lambda i, j: (i, 0))]
        out_shape = [out_shape, jax.ShapeDtypeStruct((m, DT_LANES), F32)]
    return pl.pallas_call(
        functools.partial(_in_proj_kernel, transposed=transposed, scaled_tiles=scaled_tiles,
                          scale=scale, with_dt=dt_start is not None),
        grid=(m // tm, n_total),
        in_specs=in_specs,
        out_specs=out_specs,
        out_shape=out_shape,
        scratch_shapes=[pltpu.VMEM((tm, d), BF16)],
        compiler_params=_params(("parallel", "arbitrary")),
        name="norm_in_proj",
    )(*args)


def _out_proj_kernel(*refs, n_in, final_norm):
    y_refs = refs[:n_in]
    w_refs = refs[n_in:2 * n_in]
    h_ref = refs[2 * n_in]
    o_ref = refs[-1]
    acc = h_ref[...]
    for y_ref, w_ref in zip(y_refs, w_refs):
        acc = acc + jnp.dot(y_ref[...], w_ref[...].astype(BF16), preferred_element_type=F32)
    if final_norm:
        fw_ref = refs[2 * n_in + 1]
        ms = jnp.mean(acc * acc, axis=-1, keepdims=True)
        acc = acc * lax.rsqrt(ms + NORM_EPS) * fw_ref[...]
    o_ref[...] = acc


def _out_proj(ys, ws, h, final_w, *, tm, tn, in_place):
    m, n = h.shape
    n_in = len(ys)
    final_norm = final_w is not None
    if final_norm:
        assert tn == n

    def w_spec(kdim, layer, row_block):
        return pl.BlockSpec((None, kdim, tn), lambda i, j: (layer, row_block, j))

    in_specs = [pl.BlockSpec((tm, y.shape[1]), lambda i, j: (i, 0)) for y in ys]
    in_specs += [w_spec(y.shape[1], layer, rb) for y, (_, layer, rb) in zip(ys, ws)]
    in_specs += [pl.BlockSpec((tm, tn), lambda i, j: (i, j))]
    args = list(ys) + [w for w, _, _ in ws] + [h]
    if final_norm:
        in_specs += [pl.BlockSpec((1, n), lambda i, j: (0, 0))]
        args += [final_w.reshape(1, n)]
    return pl.pallas_call(
        functools.partial(_out_proj_kernel, n_in=n_in, final_norm=final_norm),
        grid=(m // tm, n // tn),
        in_specs=in_specs,
        out_specs=pl.BlockSpec((tm, tn), lambda i, j: (i, j)),
        out_shape=jax.ShapeDtypeStruct((m, n), F32),
        input_output_aliases={2 * n_in: 0} if in_place else {},
        compiler_params=_params(("parallel", "parallel")),
        name="out_proj",
    )(*args)


def _causal_conv(ext_ref, tail_ref, cur, w_ref, b_ref):
    ext_ref[0:8, :] = tail_ref[...]
    ext_ref[8:8 + CHUNK, :] = cur
    tail_ref[...] = cur[CHUNK - 8:, :]
    acc = b_ref[...] + w_ref[CONV_WIDTH - 1:CONV_WIDTH, :] * cur
    for k in range(CONV_WIDTH - 1):
        d = CONV_WIDTH - 1 - k
        acc = acc + w_ref[k:k + 1, :] * ext_ref[8 - d:8 - d + CHUNK, :]
    return acc


def _ssd_kernel(z_ref, x_ref, bc_ref, dt_ref, cwx_ref, cbx_ref, cwbc_ref, cbbc_ref,
                dtb_ref, alog_ref, dskip_ref, nw_ref, e64_ref, e128_ref,
                o_ref,
                state_ref, xtail_ref, bctail_ref, xext_ref, bcext_ref, y_ref):
    c = pl.program_id(1)

    @pl.when(c == 0)
    def _():
        state_ref[...] = jnp.zeros_like(state_ref)
        xtail_ref[...] = jnp.zeros_like(xtail_ref)
        bctail_ref[...] = jnp.zeros_like(bctail_ref)

    L = CHUNK
    row = lax.broadcasted_iota(jnp.int32, (L, L), 0)
    col = lax.broadcasted_iota(jnp.int32, (L, L), 1)
    causal = row >= col
    tri = jnp.where(causal, 1.0, 0.0).astype(BF16)
    head_lane = lax.broadcasted_iota(jnp.int32, (L, DT_LANES), 1) < SSD_HEADS

    dt_in = dt_ref[...] + dtb_ref[...]
    dt = jnp.where(head_lane, jnp.maximum(dt_in, 0.0) + jnp.log1p(jnp.exp(-jnp.abs(dt_in))), 0.0)
    a_dt = dt * (-jnp.exp(alog_ref[...]))
    p1, p2, p3 = _split3(a_dt)
    cs = (jnp.dot(tri, p1, preferred_element_type=F32)
          + jnp.dot(tri, p2, preferred_element_type=F32)
          + jnp.dot(tri, p3, preferred_element_type=F32))
    cs_t = cs.T

    def lane_pieces(v):
        q1, q2, q3 = (q.astype(F32) for q in _split3(v))
        return (q1 + pltpu.roll(q2, SSD_HEADS, 1) + pltpu.roll(q3, 2 * SSD_HEADS, 1)).astype(BF16)

    dt_p = lane_pieces(dt)
    cs_p = lane_pieces(cs)
    e64 = e64_ref[...]
    dt_e = jnp.dot(dt_p, e64, preferred_element_type=F32)
    cs_e = jnp.dot(cs_p, e64, preferred_element_type=F32)
    cs_col = jnp.dot(cs_p, e128_ref[...], preferred_element_type=F32)

    xs = _silu(_causal_conv(xext_ref, xtail_ref, x_ref[...].astype(F32), cwx_ref, cbx_ref))
    bc = _silu(_causal_conv(bcext_ref, bctail_ref, bc_ref[...].astype(F32), cwbc_ref, cbbc_ref))

    cs_last = cs_e[L - 1:L, :]
    x_dt = xs * dt_e
    x_dt_b = x_dt.astype(BF16)
    x_end_b = (x_dt * jnp.exp(cs_last - cs_e)).astype(BF16)
    in_decay = jnp.exp(cs_e)
    chunk_decay = jnp.exp(cs_last)

    lane_l = lax.broadcasted_iota(jnp.int32, (L, 2 * SSD_HEAD_DIM), 1)
    first_head = lane_l < SSD_HEAD_DIM
    ssq = jnp.zeros((L, 1), F32)
    for g in range(SSD_GROUPS):
        gs = slice(g * GROUP_COLS, (g + 1) * GROUP_COLS)
        b_g = bc[:, g * SSD_STATE:(g + 1) * SSD_STATE].astype(BF16)
        c_g = bc[:, (SSD_GROUPS + g) * SSD_STATE:(SSD_GROUPS + g + 1) * SSD_STATE].astype(BF16)
        scores = lax.dot_general(c_g, b_g, (((1,), (1,)), ((), ())),
                                 preferred_element_type=F32)
        heads_per_group = SSD_HEADS // SSD_GROUPS
        y_parts = []
        for pair in range(heads_per_group // 2):
            ms = []
            for r in range(2):
                h = g * heads_per_group + 2 * pair + r
                seg = cs_col[:, h * L:(h + 1) * L] - cs_t[h:h + 1, :]
                decay = jnp.exp(jnp.where(causal, seg, -jnp.inf))
                ms.append((scores * decay).astype(BF16))
            lhs = jnp.concatenate(ms, axis=1)
            c0 = g * GROUP_COLS + pair * 2 * SSD_HEAD_DIM
            xp = x_dt_b[:, c0:c0 + 2 * SSD_HEAD_DIM]
            zero = jnp.zeros_like(xp)
            rhs = jnp.concatenate([jnp.where(first_head, xp, zero),
                                   jnp.where(first_head, zero, xp)], axis=0)
            y_parts.append(jnp.dot(lhs, rhs, preferred_element_type=F32))
        y_diag = jnp.concatenate(y_parts, axis=1)

        st = state_ref[:, gs]
        y_off = jnp.dot(c_g, st.astype(BF16), preferred_element_type=F32) * in_decay[:, gs]
        new_st = lax.dot_general(b_g, x_end_b[:, gs], (((0,), (0,)), ((), ())),
                                 preferred_element_type=F32)
        state_ref[:, gs] = chunk_decay[:, gs] * st + new_st

        y = y_diag + y_off + xs[:, gs] * dskip_ref[:, gs]
        y = y * _silu(z_ref[:, gs].astype(F32))
        ssq = ssq + jnp.sum(y * y, axis=-1, keepdims=True)
        y_ref[:, gs] = y

    inv = lax.rsqrt(ssq * (1.0 / D_INNER) + NORM_EPS)
    o_ref[...] = (y_ref[...] * inv * nw_ref[...]).astype(o_ref.dtype)


def _ssd_mixer(proj, dt_raw, conv_w, conv_b, dt_bias, a_log, d_skip, norm_w, *, batch, seq):
    nc = seq // CHUNK
    t = batch * seq
    lanes = jnp.arange(DT_LANES)
    valid = lanes < DT_COPIES * SSD_HEADS
    head_of_lane = lanes % SSD_HEADS

    def tile_heads(v):
        return jnp.pad(v.astype(F32), (0, DT_LANES - SSD_HEADS)).reshape(1, DT_LANES)

    def expand(width):
        cols = jnp.arange(SSD_HEADS * width) // width
        return ((head_of_lane[:, None] == cols[None, :]) & valid[:, None]).astype(BF16)

    e64 = expand(SSD_HEAD_DIM)
    e128 = expand(CHUNK)
    d_skip_e = jnp.repeat(d_skip, SSD_HEAD_DIM).reshape(1, D_INNER)

    tok = lambda b, c: b * nc + c
    const = lambda b, c: (0, 0)
    full = lambda a: pl.BlockSpec(a.shape, const)
    cwx, cwbc = conv_w[:, :D_INNER], conv_w[:, D_INNER:]
    cbx, cbbc = conv_b[:D_INNER].reshape(1, -1), conv_b[D_INNER:].reshape(1, -1)
    dtb, alog = tile_heads(dt_bias), tile_heads(a_log)
    nw = norm_w.reshape(1, D_INNER)
    params = [cwx, cbx, cwbc, cbbc, dtb, alog, d_skip_e, nw, e64, e128]
    return pl.pallas_call(
        _ssd_kernel,
        grid=(batch, nc),
        in_specs=[
            pl.BlockSpec((CHUNK, D_INNER), lambda b, c: (tok(b, c), 0)),
            pl.BlockSpec((CHUNK, D_INNER), lambda b, c: (tok(b, c), 4)),
            pl.BlockSpec((CHUNK, BC_DIM), lambda b, c: (tok(b, c), 10)),
            pl.BlockSpec((CHUNK, DT_LANES), lambda b, c: (tok(b, c), 0)),
        ] + [full(p) for p in params],
        out_specs=pl.BlockSpec((CHUNK, D_INNER), lambda b, c: (tok(b, c), 0)),
        out_shape=jax.ShapeDtypeStruct((t, D_INNER), BF16),
        scratch_shapes=[
            pltpu.VMEM((SSD_STATE, D_INNER), F32),
            pltpu.VMEM((8, D_INNER), F32),
            pltpu.VMEM((8, BC_DIM), F32),
            pltpu.VMEM((8 + CHUNK, D_INNER), F32),
            pltpu.VMEM((8 + CHUNK, BC_DIM), F32),
            pltpu.VMEM((CHUNK, D_INNER), F32),
        ],
        compiler_params=_params(("parallel", "arbitrary")),
        name="ssd_mixer",
    )(proj, proj, proj, dt_raw, *params)


def _sgu_kernel(g_ref, u_ref, v_ref, lnw_ref, lnb_ref, ws_ref, bias_ref, o_ref, wtri_ref):
    @pl.when(pl.program_id(0) == 0)
    def _():
        row = lax.broadcasted_iota(jnp.int32, (CHUNK, CHUNK), 0)
        col = lax.broadcasted_iota(jnp.int32, (CHUNK, CHUNK), 1)
        for g in range(SGU_GROUPS):
            wtri_ref[g] = jnp.where(row >= col, ws_ref[g], 0.0).astype(BF16)

    v = _gelu_tanh(v_ref[...].astype(F32))
    mu = jnp.mean(v, axis=-1, keepdims=True)
    vc = v - mu
    var = jnp.mean(vc * vc, axis=-1, keepdims=True)
    vn = (vc * lax.rsqrt(var + NORM_EPS) * lnw_ref[...] + lnb_ref[...]).astype(BF16)
    for g in range(SGU_GROUPS):
        gs = slice(g * SGU_GROUP_DIM, (g + 1) * SGU_GROUP_DIM)
        mixed = jnp.dot(wtri_ref[g], vn[:, gs], preferred_element_type=F32) + bias_ref[:, gs]
        u = _gelu_tanh(u_ref[:, gs].astype(F32))
        o_ref[:, gs] = (u * mixed * _silu(g_ref[:, gs].astype(F32))).astype(o_ref.dtype)


def _sgu_mixer(proj, ln_w, ln_b, w_s, b_s):
    t = proj.shape[0]
    bias = jnp.repeat(b_s.T, SGU_GROUP_DIM, axis=1)
    const2 = lambda i: (0, 0)
    return pl.pallas_call(
        _sgu_kernel,
        grid=(t // CHUNK,),
        in_specs=[
            pl.BlockSpec((CHUNK, D_INNER), lambda i: (i, 1)),
            pl.BlockSpec((CHUNK, D_INNER), lambda i: (i, 2)),
            pl.BlockSpec((CHUNK, D_INNER), lambda i: (i, 3)),
            pl.BlockSpec((1, D_INNER), const2),
            pl.BlockSpec((1, D_INNER), const2),
            pl.BlockSpec(w_s.shape, lambda i: (0, 0, 0)),
            pl.BlockSpec((CHUNK, D_INNER), const2),
        ],
        out_specs=pl.BlockSpec((CHUNK, D_INNER), lambda i: (i, 0)),
        out_shape=jax.ShapeDtypeStruct((t, D_INNER), BF16),
        scratch_shapes=[pltpu.VMEM((SGU_GROUPS, CHUNK, CHUNK), BF16)],
        compiler_params=_params(("arbitrary",)),
        name="sgu_mixer",
    )(proj, proj, proj, ln_w.reshape(1, -1), ln_b.reshape(1, -1), w_s, bias)


def _attn_kernel(q_ref, k_ref, v_ref, g_ref, o_ref, acc_ref, carry_ref):
    i = pl.program_id(2)
    tq, tk = ATTN_TQ, ATTN_TK
    srow = lax.broadcasted_iota(jnp.int32, (tk, tk), 0)
    scol = lax.broadcasted_iota(jnp.int32, (tk, tk), 1)
    later = jnp.where(scol > srow, 1.0, 0.0).astype(BF16)

    def softplus2(z):
        return jnp.maximum(z, 0.0) + jnp.log2(1.0 + jnp.exp2(-jnp.abs(z)))

    def later_sums(sp, nsub):
        hi = sp.astype(BF16)
        afts, tots = [], []
        for r in range(nsub):
            aft = jnp.dot(later, hi[r * tk:(r + 1) * tk], preferred_element_type=F32)
            afts.append(aft)
            tots.append(aft[0:1, :] + sp[r * tk:r * tk + 1, :])
        return afts, tots

    def block_weights(z, sp, afts, tots, carry, nsub):
        ws = [None] * nsub
        for r in reversed(range(nsub)):
            rs = slice(r * tk, (r + 1) * tk)
            ws[r] = jnp.exp2((z[rs] - sp[rs]) - afts[r] - carry).astype(BF16)
            carry = carry + tots[r]
        return jnp.concatenate(ws, axis=0), carry

    def values(ks, nkeys, w):
        return lax.dot_general(v_ref[pl.ds(ks, nkeys), :], w, (((0,), (0,)), ((), ())),
                               preferred_element_type=F32)

    tiles = range(ATTN_GROUP)
    nsub = tq // tk
    tile_idx = [i * ATTN_GROUP + g for g in tiles]
    diag_start = [pl.multiple_of(t * tq, tq) for t in tile_idx]
    prev_start = [pl.multiple_of(jnp.maximum(t - 1, 0) * tq, tq) for t in tile_idx]
    keep = [jnp.where(t > 0, 1.0, 0.0) for t in tile_idx]
    below = (lax.broadcasted_iota(jnp.int32, (tq, tq), 0)
             < lax.broadcasted_iota(jnp.int32, (tq, tq), 1))

    def scores(g, ks):
        return lax.dot_general(k_ref[pl.ds(ks, tq), :], q_ref[g * tq:(g + 1) * tq, :],
                               (((1,), (1,)), ((), ())), preferred_element_type=F32)

    z_diag = [scores(g, diag_start[g]) for g in tiles]
    z_prev = [scores(g, prev_start[g]) for g in tiles]
    sp_diag = [jnp.where(below, softplus2(z), 0.0) for z in z_diag]
    sp_prev = [softplus2(z_prev[g]) * keep[g] for g in tiles]
    sums_diag = [later_sums(sp, nsub) for sp in sp_diag]
    sums_prev = [later_sums(sp, nsub) for sp in sp_prev]
    for g in tiles:
        w, carry = block_weights(z_diag[g], sp_diag[g], *sums_diag[g], jnp.zeros((1, tq), F32), nsub)
        w = jnp.where(below, w, jnp.zeros_like(w))
        acc = values(diag_start[g], tq, w)
        w, carry = block_weights(z_prev[g], sp_prev[g], *sums_prev[g], carry, nsub)
        acc_ref[g] = acc + values(prev_start[g], tq, w * keep[g].astype(BF16))
        carry_ref[g] = carry

    for g in tiles:
        def more(state):
            j, least = state
            return jnp.logical_and(j >= 0, least < ATTN_SKIP_LOG2)

        def earlier(state, g=g):
            j, _ = state
            ks = pl.multiple_of(j * tq, tq)
            z = lax.dot_general(k_ref[pl.ds(ks, tq), :], q_ref[g * tq:(g + 1) * tq, :],
                                (((1,), (1,)), ((), ())), preferred_element_type=F32)
            sp = softplus2(z)
            afts, tots = later_sums(sp, tq // tk)
            w, carry = block_weights(z, sp, afts, tots, carry_ref[g], tq // tk)
            acc_ref[g] += values(ks, tq, w)
            carry_ref[g] = carry
            return j - 1, jnp.min(carry)

        lax.while_loop(more, earlier, (tile_idx[g] - 2, jnp.min(carry_ref[g])))

    for g in tiles:
        rows = slice(g * tq, (g + 1) * tq)
        o_ref[rows, :] = (acc_ref[g].T * _silu(g_ref[rows, :].astype(F32))).astype(o_ref.dtype)


def _attention(qkvg, *, batch, seq):
    rows = ATTN_GROUP * ATTN_TQ
    assert seq % rows == 0
    nq = seq // rows
    t = batch * seq
    return pl.pallas_call(
        _attn_kernel,
        grid=(batch, SB_HEADS, nq),
        in_specs=[
            pl.BlockSpec((rows, SB_HEAD_DIM), lambda b, h, i: (b * nq + i, h)),
            pl.BlockSpec((seq, SB_HEAD_DIM), lambda b, h, i: (b, SB_HEADS + h)),
            pl.BlockSpec((seq, SB_HEAD_DIM), lambda b, h, i: (b, 2 * SB_HEADS + h)),
            pl.BlockSpec((rows, SB_HEAD_DIM), lambda b, h, i: (b * nq + i, 3 * SB_HEADS + h)),
        ],
        out_specs=pl.BlockSpec((rows, SB_HEAD_DIM), lambda b, h, i: (b * nq + i, h)),
        out_shape=jax.ShapeDtypeStruct((t, SB_HEADS * SB_HEAD_DIM), BF16),
        scratch_shapes=[
            pltpu.VMEM((ATTN_GROUP, SB_HEAD_DIM, ATTN_TQ), F32),
            pltpu.VMEM((ATTN_GROUP, 1, ATTN_TQ), F32),
        ],
        compiler_params=_params(("parallel", "parallel", "arbitrary")),
        name="stickbreak_attn",
    )(qkvg, qkvg, qkvg, qkvg)


EVEN_TN = 1024
EVEN_ALIGNED = 2 * D_INNER + BC_DIM
EVEN_AFTER_DT = EVEN_ALIGNED + SSD_HEADS


def _even_layer(h, norm_w, ev_w_in_t, ev_w_out, layer, conv_w, conv_b, dt_bias, a_log,
                d_skip, ssd_norm_w, sgu_ln_w, sgu_ln_b, sgu_w, sgu_b, *, batch, seq, in_place):
    tiles = lambda cols: cols // EVEN_TN
    proj, dt_raw = _in_proj(
        h, norm_w, ev_w_in_t, layer, tm=1024, tn=EVEN_TN, transposed=True,
        segments=[(tiles(D_INNER), 0, 0),
                  (tiles(D_INNER + BC_DIM), D_INNER, tiles(4 * D_INNER)),
                  (tiles(3 * D_INNER), EVEN_AFTER_DT, tiles(D_INNER))],
        dt_start=EVEN_ALIGNED)
    y_a = _ssd_mixer(proj, dt_raw, conv_w, conv_b, dt_bias, a_log, d_skip, ssd_norm_w,
                     batch=batch, seq=seq)
    y_b = _sgu_mixer(proj, sgu_ln_w, sgu_ln_b, sgu_w, sgu_b)
    return _out_proj([y_a, y_b], [(ev_w_out, layer, 0), (ev_w_out, layer, 1)], h, None,
                     tm=1024, tn=512, in_place=in_place)


def _odd_layer(h, norm_w, od_w_in, od_w_out, layer, final_w, *, batch, seq):
    scale = SB_HEAD_DIM ** -0.5 * math.log2(math.e)
    tn = 1024
    qkvg = _in_proj(h, norm_w, od_w_in, layer, tm=1024, tn=tn,
                    segments=[(4 * D_INNER // tn, 0, 0)], scaled_tiles=D_INNER // tn, scale=scale)
    y = _attention(qkvg, batch=batch, seq=seq)
    if final_w is None:
        return _out_proj([y], [(od_w_out, layer, 0)], h, None, tm=1024, tn=512, in_place=True)
    w_last = od_w_out[layer:layer + 1].astype(BF16)
    return _out_proj([y], [(w_last, 0, 0)], h, final_w, tm=512, tn=D_MODEL, in_place=True)


def kernel(x, norm_w, final_norm_w, ev_w_in, ev_conv_w, ev_conv_b, ev_dt_bias, ev_a_log,
           ev_d_skip, ev_ssd_norm_w, ev_sgu_ln_w, ev_sgu_ln_b, ev_sgu_w, ev_sgu_b, ev_w_out,
           od_w_in, od_w_out):
    batch, seq, d = x.shape
    depth = norm_w.shape[0]
    h = x.reshape(batch * seq, d)
    ev_w_in_t = jnp.swapaxes(ev_w_in, 1, 2)
    for layer in range(depth):
        i = layer // 2
        if layer % 2 == 0:
            h = _even_layer(h, norm_w[layer], ev_w_in_t, ev_w_out, i, ev_conv_w[i],
                            ev_conv_b[i], ev_dt_bias[i], ev_a_log[i], ev_d_skip[i],
                            ev_ssd_norm_w[i], ev_sgu_ln_w[i], ev_sgu_ln_b[i], ev_sgu_w[i],
                            ev_sgu_b[i], batch=batch, seq=seq, in_place=layer > 0)
        else:
            fw = final_norm_w if layer == depth - 1 else None
            h = _odd_layer(h, norm_w[layer], od_w_in, od_w_out, i, fw, batch=batch, seq=seq)
    if depth % 2 == 1:
        raise NotImplementedError("final RMSNorm is fused into an odd last layer")
    return h.reshape(batch, seq, d)
```

```python
import functools
import math

import jax
import jax.numpy as jnp
from jax import lax
from jax.experimental import pallas as pl
from jax.experimental.pallas import tpu as pltpu

F32 = jnp.float32
BF16 = jnp.bfloat16

D_MODEL = 2048
NORM_EPS = 1e-5
SSD_HEAD_DIM = 64
SSD_HEADS = 32
SSD_GROUPS = 4
SSD_STATE = 128
CHUNK = 128
CONV_WIDTH = 4
D_INNER = 2048
BC_DIM = 2 * SSD_GROUPS * SSD_STATE
GROUP_COLS = D_INNER // SSD_GROUPS
SGU_GROUPS = 16
SGU_GROUP_DIM = 128
SB_HEADS = 16
SB_HEAD_DIM = 128
ATTN_TQ = 256
ATTN_TK = 128
ATTN_GROUP = 8
ATTN_SKIP_LOG2 = 160.0
DT_LANES = 128
DT_COPIES = 3

V7X_VMEM_LIMIT = 52 * 1024 * 1024
V7X_F32_SUBLANES = 8


def _params(sem):
    return pltpu.CompilerParams(dimension_semantics=sem, vmem_limit_bytes=V7X_VMEM_LIMIT)


def _sigmoid(x):
    return 1.0 / (1.0 + jnp.exp(-x))


def _silu(x):
    return x * _sigmoid(x)


def _gelu_tanh(x):
    c = math.sqrt(2.0 / math.pi)
    return x * (0.5 * (1.0 + jnp.tanh(c * (x + 0.044715 * (x * x * x)))))


def _split3(x):
    h1 = x.astype(BF16)
    r1 = x - h1.astype(F32)
    h2 = r1.astype(BF16)
    r2 = r1 - h2.astype(F32)
    return h1, h2, r2.astype(BF16)


def _in_proj_kernel(*refs, transposed, scaled_tiles, scale, with_dt):
    x_ref, nw_ref, w_ref = refs[:3]
    hn_ref = refs[-1]
    if with_dt:
        wdt_ref, o_ref, dt_ref = refs[3:6]
    else:
        o_ref = refs[3]
    j = pl.program_id(1)

    def matmul(w_ref):
        w = w_ref[...].astype(BF16)
        if transposed:
            return lax.dot_general(hn_ref[...], w, (((1,), (1,)), ((), ())),
                                   preferred_element_type=F32)
        return jnp.dot(hn_ref[...], w, preferred_element_type=F32)

    @pl.when(j == 0)
    def _():
        x = x_ref[...]
        ms = jnp.mean(x * x, axis=-1, keepdims=True)
        hn_ref[...] = (x * lax.rsqrt(ms + NORM_EPS) * nw_ref[...]).astype(BF16)
        if with_dt:
            dt_ref[...] = matmul(wdt_ref)

    res = matmul(w_ref)
    if scaled_tiles:
        res = res * jnp.where(j < scaled_tiles, scale, 1.0)
    o_ref[...] = res.astype(o_ref.dtype)


def _in_proj(h, norm_w, w, layer, *, tm, tn, segments, transposed=False, scaled_tiles=0,
             scale=1.0, dt_start=None):
    m, d = h.shape
    n_total = sum(cnt for cnt, _, _ in segments)

    def pick(j, field, unit=1):
        out, first = 0, 0
        for seg in segments:
            inside = jnp.logical_and(j >= first, j < first + seg[0])
            step = tn // unit if field == 1 else 1
            out = out + jnp.where(inside, (j - first) * step + seg[field] // (unit if field == 1 else 1), 0)
            first += seg[0]
        return out

    if transposed:
        n_cols = w.shape[1]
        w = w.reshape(-1, d)
        sub = V7X_F32_SUBLANES
        assert n_cols % sub == 0 and all(first % sub == 0 for _, first, _ in segments)
        w_spec = pl.BlockSpec((pl.Element(tn), pl.Element(d)),
                              lambda i, j: (sub * (layer * n_cols // sub + pick(j, 1, sub)), 0))
    else:
        assert all(first % tn == 0 for _, first, _ in segments)
        w_spec = pl.BlockSpec((None, d, tn), lambda i, j: (layer, 0, pick(j, 1, tn)))
    in_specs = [pl.BlockSpec((tm, d), lambda i, j: (i, 0)),
                pl.BlockSpec((1, d), lambda i, j: (0, 0)),
                w_spec]
    args = [h, norm_w.reshape(1, d), w]
    out_specs = pl.BlockSpec((tm, tn), lambda i, j: (i, pick(j, 2)))
    out_shape = jax.ShapeDtypeStruct((m, n_total * tn), BF16)
    if dt_start is not None:
        assert transposed
        in_specs.append(pl.BlockSpec((pl.Element(DT_LANES), pl.Element(d)),
                                     lambda i, j: (sub * ((layer * n_cols + dt_start) // sub), 0)))
        args.append(w)
        out_specs = [out_specs, pl.BlockSpec((tm, DT_LANES), lambda i, j: (i, 0))]
        out_shape = [out_shape, jax.ShapeDtypeStruct((m, DT_LANES), F32)]
    return pl.pallas_call(
        functools.partial(_in_proj_kernel, transposed=transposed, scaled_tiles=scaled_tiles,
                          scale=scale, with_dt=dt_start is not None),
        grid=(m // tm, n_total),
        in_specs=in_specs,
        out_specs=out_specs,
        out_shape=out_shape,
        scratch_shapes=[pltpu.VMEM((tm, d), BF16)],
        compiler_params=_params(("parallel", "arbitrary")),
        name="norm_in_proj",
    )(*args)


def _out_proj_kernel(*refs, n_in, final_norm):
    y_refs = refs[:n_in]
    w_refs = refs[n_in:2 * n_in]
    h_ref = refs[2 * n_in]
    o_ref, wb_ref = refs[-2:]

    @pl.when(pl.program_id(1) == 0)
    def _():
        for k, w_ref in enumerate(w_refs):
            wb_ref[k] = w_ref[...].astype(BF16)

    acc = h_ref[...]
    for k, y_ref in enumerate(y_refs):
        acc = acc + jnp.dot(y_ref[...], wb_ref[k], preferred_element_type=F32)
    if final_norm:
        fw_ref = refs[2 * n_in + 1]
        ms = jnp.mean(acc * acc, axis=-1, keepdims=True)
        acc = acc * lax.rsqrt(ms + NORM_EPS) * fw_ref[...]
    o_ref[...] = acc


def _out_proj(ys, ws, h, final_w, *, tm, tn, in_place):
    m, n = h.shape
    n_in = len(ys)
    kdim = ys[0].shape[1]
    assert all(y.shape[1] == kdim for y in ys)
    final_norm = final_w is not None
    if final_norm:
        assert tn == n

    def w_spec(layer, row_block):
        return pl.BlockSpec((None, kdim, tn), lambda j, i: (layer, row_block, j),
                            pipeline_mode=pl.Buffered(1))

    in_specs = [pl.BlockSpec((tm, kdim), lambda j, i: (i, 0)) for _ in ys]
    in_specs += [w_spec(layer, rb) for _, layer, rb in ws]
    in_specs += [pl.BlockSpec((tm, tn), lambda j, i: (i, j))]
    args = list(ys) + [w for w, _, _ in ws] + [h]
    if final_norm:
        in_specs += [pl.BlockSpec((1, n), lambda j, i: (0, 0))]
        args += [final_w.reshape(1, n)]
    return pl.pallas_call(
        functools.partial(_out_proj_kernel, n_in=n_in, final_norm=final_norm),
        grid=(n // tn, m // tm),
        in_specs=in_specs,
        out_specs=pl.BlockSpec((tm, tn), lambda j, i: (i, j)),
        out_shape=jax.ShapeDtypeStruct((m, n), F32),
        scratch_shapes=[pltpu.VMEM((n_in, kdim, tn), BF16)],
        input_output_aliases={2 * n_in: 0} if in_place else {},
        compiler_params=_params(("arbitrary", "arbitrary")),
        name="out_proj",
    )(*args)


def _causal_conv(ext_ref, tail_ref, cur, w_ref, b_ref):
    ext_ref[0:8, :] = tail_ref[...]
    ext_ref[8:8 + CHUNK, :] = cur
    tail_ref[...] = cur[CHUNK - 8:, :]
    acc = b_ref[...] + w_ref[CONV_WIDTH - 1:CONV_WIDTH, :] * cur
    for k in range(CONV_WIDTH - 1):
        d = CONV_WIDTH - 1 - k
        acc = acc + w_ref[k:k + 1, :] * ext_ref[8 - d:8 - d + CHUNK, :]
    return acc


def _ssd_kernel(z_ref, x_ref, bc_ref, dt_ref, cwx_ref, cbx_ref, cwbc_ref, cbbc_ref,
                dtb_ref, alog_ref, dskip_ref, nw_ref, e64_ref, e128_ref,
                o_ref,
                state_ref, xtail_ref, bctail_ref, xext_ref, bcext_ref, y_ref):
    c = pl.program_id(1)

    @pl.when(c == 0)
    def _():
        state_ref[...] = jnp.zeros_like(state_ref)
        xtail_ref[...] = jnp.zeros_like(xtail_ref)
        bctail_ref[...] = jnp.zeros_like(bctail_ref)

    L = CHUNK
    row = lax.broadcasted_iota(jnp.int32, (L, L), 0)
    col = lax.broadcasted_iota(jnp.int32, (L, L), 1)
    causal = row >= col
    tri = jnp.where(causal, 1.0, 0.0).astype(BF16)
    head_lane = lax.broadcasted_iota(jnp.int32, (L, DT_LANES), 1) < SSD_HEADS

    dt_in = dt_ref[...] + dtb_ref[...]
    dt = jnp.where(head_lane, jnp.maximum(dt_in, 0.0) + jnp.log1p(jnp.exp(-jnp.abs(dt_in))), 0.0)
    a_dt = dt * (-jnp.exp(alog_ref[...]))
    p1, p2, p3 = _split3(a_dt)
    cs = (jnp.dot(tri, p1, preferred_element_type=F32)
          + jnp.dot(tri, p2, preferred_element_type=F32)
          + jnp.dot(tri, p3, preferred_element_type=F32))
    cs_t = cs.T

    def lane_pieces(v):
        q1, q2, q3 = (q.astype(F32) for q in _split3(v))
        return (q1 + pltpu.roll(q2, SSD_HEADS, 1) + pltpu.roll(q3, 2 * SSD_HEADS, 1)).astype(BF16)

    dt_p = lane_pieces(dt)
    cs_p = lane_pieces(cs)
    e64 = e64_ref[...]
    dt_e = jnp.dot(dt_p, e64, preferred_element_type=F32)
    cs_e = jnp.dot(cs_p, e64, preferred_element_type=F32)
    cs_col = jnp.dot(cs_p, e128_ref[...], preferred_element_type=F32)

    xs = _silu(_causal_conv(xext_ref, xtail_ref, x_ref[...].astype(F32), cwx_ref, cbx_ref))
    bc = _silu(_causal_conv(bcext_ref, bctail_ref, bc_ref[...].astype(F32), cwbc_ref, cbbc_ref))

    cs_last = cs_e[L - 1:L, :]
    x_dt = xs * dt_e
    x_dt_b = x_dt.astype(BF16)
    x_end_b = (x_dt * jnp.exp(cs_last - cs_e)).astype(BF16)
    in_decay = jnp.exp(cs_e)
    chunk_decay = jnp.exp(cs_last)

    lane_l = lax.broadcasted_iota(jnp.int32, (L, 2 * SSD_HEAD_DIM), 1)
    first_head = lane_l < SSD_HEAD_DIM
    ssq = jnp.zeros((L, 1), F32)
    for g in range(SSD_GROUPS):
        gs = slice(g * GROUP_COLS, (g + 1) * GROUP_COLS)
        b_g = bc[:, g * SSD_STATE:(g + 1) * SSD_STATE].astype(BF16)
        c_g = bc[:, (SSD_GROUPS + g) * SSD_STATE:(SSD_GROUPS + g + 1) * SSD_STATE].astype(BF16)
        scores = lax.dot_general(c_g, b_g, (((1,), (1,)), ((), ())),
                                 preferred_element_type=F32)
        heads_per_group = SSD_HEADS // SSD_GROUPS
        y_parts = []
        for pair in range(heads_per_group // 2):
            ms = []
            for r in range(2):
                h = g * heads_per_group + 2 * pair + r
                seg = cs_col[:, h * L:(h + 1) * L] - cs_t[h:h + 1, :]
                decay = jnp.exp(jnp.where(causal, seg, -jnp.inf))
                ms.append((scores * decay).astype(BF16))
            lhs = jnp.concatenate(ms, axis=1)
            c0 = g * GROUP_COLS + pair * 2 * SSD_HEAD_DIM
            xp = x_dt_b[:, c0:c0 + 2 * SSD_HEAD_DIM]
            zero = jnp.zeros_like(xp)
            rhs = jnp.concatenate([jnp.where(first_head, xp, zero),
                                   jnp.where(first_head, zero, xp)], axis=0)
            y_parts.append(jnp.dot(lhs, rhs, preferred_element_type=F32))
        y_diag = jnp.concatenate(y_parts, axis=1)

        st = state_ref[:, gs]
        y_off = jnp.dot(c_g, st.astype(BF16), preferred_element_type=F32) * in_decay[:, gs]
        new_st = lax.dot_general(b_g, x_end_b[:, gs], (((0,), (0,)), ((), ())),
                                 preferred_element_type=F32)
        state_ref[:, gs] = chunk_decay[:, gs] * st + new_st

        y = y_diag + y_off + xs[:, gs] * dskip_ref[:, gs]
        y = y * _silu(z_ref[:, gs].astype(F32))
        ssq = ssq + jnp.sum(y * y, axis=-1, keepdims=True)
        y_ref[:, gs] = y

    inv = lax.rsqrt(ssq * (1.0 / D_INNER) + NORM_EPS)
    o_ref[...] = (y_ref[...] * inv * nw_ref[...]).astype(o_ref.dtype)


def _ssd_mixer(proj, dt_raw, conv_w, conv_b, dt_bias, a_log, d_skip, norm_w, *, batch, seq):
    nc = seq // CHUNK
    t = batch * seq
    lanes = jnp.arange(DT_LANES)
    valid = lanes < DT_COPIES * SSD_HEADS
    head_of_lane = lanes % SSD_HEADS

    def tile_heads(v):
        return jnp.pad(v.astype(F32), (0, DT_LANES - SSD_HEADS)).reshape(1, DT_LANES)

    def expand(width):
        cols = jnp.arange(SSD_HEADS * width) // width
        return ((head_of_lane[:, None] == cols[None, :]) & valid[:, None]).astype(BF16)

    e64 = expand(SSD_HEAD_DIM)
    e128 = expand(CHUNK)
    d_skip_e = jnp.repeat(d_skip, SSD_HEAD_DIM).reshape(1, D_INNER)

    tok = lambda b, c: b * nc + c
    const = lambda b, c: (0, 0)
    full = lambda a: pl.BlockSpec(a.shape, const)
    cwx, cwbc = conv_w[:, :D_INNER], conv_w[:, D_INNER:]
    cbx, cbbc = conv_b[:D_INNER].reshape(1, -1), conv_b[D_INNER:].reshape(1, -1)
    dtb, alog = tile_heads(dt_bias), tile_heads(a_log)
    nw = norm_w.reshape(1, D_INNER)
    params = [cwx, cbx, cwbc, cbbc, dtb, alog, d_skip_e, nw, e64, e128]
    return pl.pallas_call(
        _ssd_kernel,
        grid=(batch, nc),
        in_specs=[
            pl.BlockSpec((CHUNK, D_INNER), lambda b, c: (tok(b, c), 0)),
            pl.BlockSpec((CHUNK, D_INNER), lambda b, c: (tok(b, c), 4)),
            pl.BlockSpec((CHUNK, BC_DIM), lambda b, c: (tok(b, c), 10)),
            pl.BlockSpec((CHUNK, DT_LANES), lambda b, c: (tok(b, c), 0)),
        ] + [full(p) for p in params],
        out_specs=pl.BlockSpec((CHUNK, D_INNER), lambda b, c: (tok(b, c), 0)),
        out_shape=jax.ShapeDtypeStruct((t, D_INNER), BF16),
        scratch_shapes=[
            pltpu.VMEM((SSD_STATE, D_INNER), F32),
            pltpu.VMEM((8, D_INNER), F32),
            pltpu.VMEM((8, BC_DIM), F32),
            pltpu.VMEM((8 + CHUNK, D_INNER), F32),
            pltpu.VMEM((8 + CHUNK, BC_DIM), F32),
            pltpu.VMEM((CHUNK, D_INNER), F32),
        ],
        compiler_params=_params(("parallel", "arbitrary")),
        name="ssd_mixer",
    )(proj, proj, proj, dt_raw, *params)


def _sgu_kernel(g_ref, u_ref, v_ref, lnw_ref, lnb_ref, ws_ref, bias_ref, o_ref, wtri_ref):
    @pl.when(pl.program_id(0) == 0)
    def _():
        row = lax.broadcasted_iota(jnp.int32, (CHUNK, CHUNK), 0)
        col = lax.broadcasted_iota(jnp.int32, (CHUNK, CHUNK), 1)
        for g in range(SGU_GROUPS):
            wtri_ref[g] = jnp.where(row >= col, ws_ref[g], 0.0).astype(BF16)

    v = _gelu_tanh(v_ref[...].astype(F32))
    mu = jnp.mean(v, axis=-1, keepdims=True)
    vc = v - mu
    var = jnp.mean(vc * vc, axis=-1, keepdims=True)
    vn = (vc * lax.rsqrt(var + NORM_EPS) * lnw_ref[...] + lnb_ref[...]).astype(BF16)
    for g in range(SGU_GROUPS):
        gs = slice(g * SGU_GROUP_DIM, (g + 1) * SGU_GROUP_DIM)
        mixed = jnp.dot(wtri_ref[g], vn[:, gs], preferred_element_type=F32) + bias_ref[:, gs]
        u = _gelu_tanh(u_ref[:, gs].astype(F32))
        o_ref[:, gs] = (u * mixed * _silu(g_ref[:, gs].astype(F32))).astype(o_ref.dtype)


def _sgu_mixer(proj, ln_w, ln_b, w_s, b_s):
    t = proj.shape[0]
    bias = jnp.repeat(b_s.T, SGU_GROUP_DIM, axis=1)
    const2 = lambda i: (0, 0)
    return pl.pallas_call(
        _sgu_kernel,
        grid=(t // CHUNK,),
        in_specs=[
            pl.BlockSpec((CHUNK, D_INNER), lambda i: (i, 1)),
            pl.BlockSpec((CHUNK, D_INNER), lambda i: (i, 2)),
            pl.BlockSpec((CHUNK, D_INNER), lambda i: (i, 3)),
            pl.BlockSpec((1, D_INNER), const2),
            pl.BlockSpec((1, D_INNER), const2),
            pl.BlockSpec(w_s.shape, lambda i: (0, 0, 0)),
            pl.BlockSpec((CHUNK, D_INNER), const2),
        ],
        out_specs=pl.BlockSpec((CHUNK, D_INNER), lambda i: (i, 0)),
        out_shape=jax.ShapeDtypeStruct((t, D_INNER), BF16),
        scratch_shapes=[pltpu.VMEM((SGU_GROUPS, CHUNK, CHUNK), BF16)],
        compiler_params=_params(("arbitrary",)),
        name="sgu_mixer",
    )(proj, proj, proj, ln_w.reshape(1, -1), ln_b.reshape(1, -1), w_s, bias)


def _attn_kernel(q_ref, k_ref, v_ref, g_ref, o_ref, acc_ref, carry_ref):
    i = pl.program_id(2)
    tq, tk = ATTN_TQ, ATTN_TK
    srow = lax.broadcasted_iota(jnp.int32, (tk, tk), 0)
    scol = lax.broadcasted_iota(jnp.int32, (tk, tk), 1)
    later = jnp.where(scol > srow, 1.0, 0.0).astype(BF16)

    def softplus2(z):
        return jnp.maximum(z, 0.0) + jnp.log2(1.0 + jnp.exp2(-jnp.abs(z)))

    def later_sums(sp, nsub):
        hi = sp.astype(BF16)
        afts, tots = [], []
        for r in range(nsub):
            aft = jnp.dot(later, hi[r * tk:(r + 1) * tk], preferred_element_type=F32)
            afts.append(aft)
            tots.append(aft[0:1, :] + sp[r * tk:r * tk + 1, :])
        return afts, tots

    def block_weights(z, sp, afts, tots, carry, nsub):
        ws = [None] * nsub
        for r in reversed(range(nsub)):
            rs = slice(r * tk, (r + 1) * tk)
            ws[r] = jnp.exp2((z[rs] - sp[rs]) - afts[r] - carry).astype(BF16)
            carry = carry + tots[r]
        return jnp.concatenate(ws, axis=0), carry

    def values(ks, nkeys, w):
        return lax.dot_general(v_ref[pl.ds(ks, nkeys), :], w, (((0,), (0,)), ((), ())),
                               preferred_element_type=F32)

    tiles = range(ATTN_GROUP)
    nsub = tq // tk
    tile_idx = [i * ATTN_GROUP + g for g in tiles]
    diag_start = [pl.multiple_of(t * tq, tq) for t in tile_idx]
    prev_start = [pl.multiple_of(jnp.maximum(t - 1, 0) * tq, tq) for t in tile_idx]
    keep = [jnp.where(t > 0, 1.0, 0.0) for t in tile_idx]
    below = (lax.broadcasted_iota(jnp.int32, (tq, tq), 0)
             < lax.broadcasted_iota(jnp.int32, (tq, tq), 1))

    def scores(g, ks):
        return lax.dot_general(k_ref[pl.ds(ks, tq), :], q_ref[g * tq:(g + 1) * tq, :],
                               (((1,), (1,)), ((), ())), preferred_element_type=F32)

    z_diag = [scores(g, diag_start[g]) for g in tiles]
    z_prev = [scores(g, prev_start[g]) for g in tiles]
    sp_diag = [jnp.where(below, softplus2(z), 0.0) for z in z_diag]
    sp_prev = [softplus2(z_prev[g]) * keep[g] for g in tiles]
    sums_diag = [later_sums(sp, nsub) for sp in sp_diag]
    sums_prev = [later_sums(sp, nsub) for sp in sp_prev]
    for g in tiles:
        w, carry = block_weights(z_diag[g], sp_diag[g], *sums_diag[g], jnp.zeros((1, tq), F32), nsub)
        w = jnp.where(below, w, jnp.zeros_like(w))
        acc = values(diag_start[g], tq, w)
        w, carry = block_weights(z_prev[g], sp_prev[g], *sums_prev[g], carry, nsub)
        acc_ref[g] = acc + values(prev_start[g], tq, w * keep[g].astype(BF16))
        carry_ref[g] = carry

    for g in tiles:
        def more(state):
            j, least = state
            return jnp.logical_and(j >= 0, least < ATTN_SKIP_LOG2)

        def earlier(state, g=g):
            j, _ = state
            ks = pl.multiple_of(j * tq, tq)
            z = lax.dot_general(k_ref[pl.ds(ks, tq), :], q_ref[g * tq:(g + 1) * tq, :],
                                (((1,), (1,)), ((), ())), preferred_element_type=F32)
            sp = softplus2(z)
            afts, tots = later_sums(sp, tq // tk)
            w, carry = block_weights(z, sp, afts, tots, carry_ref[g], tq // tk)
            acc_ref[g] += values(ks, tq, w)
            carry_ref[g] = carry
            return j - 1, jnp.min(carry)

        lax.while_loop(more, earlier, (tile_idx[g] - 2, jnp.min(carry_ref[g])))

    for g in tiles:
        rows = slice(g * tq, (g + 1) * tq)
        o_ref[rows, :] = (acc_ref[g].T * _silu(g_ref[rows, :].astype(F32))).astype(o_ref.dtype)


def _attention(qkvg, *, batch, seq):
    rows = ATTN_GROUP * ATTN_TQ
    assert seq % rows == 0
    nq = seq // rows
    t = batch * seq
    return pl.pallas_call(
        _attn_kernel,
        grid=(batch, SB_HEADS, nq),
        in_specs=[
            pl.BlockSpec((rows, SB_HEAD_DIM), lambda b, h, i: (b * nq + i, h)),
            pl.BlockSpec((seq, SB_HEAD_DIM), lambda b, h, i: (b, SB_HEADS + h)),
            pl.BlockSpec((seq, SB_HEAD_DIM), lambda b, h, i: (b, 2 * SB_HEADS + h)),
            pl.BlockSpec((rows, SB_HEAD_DIM), lambda b, h, i: (b * nq + i, 3 * SB_HEADS + h)),
        ],
        out_specs=pl.BlockSpec((rows, SB_HEAD_DIM), lambda b, h, i: (b * nq + i, h)),
        out_shape=jax.ShapeDtypeStruct((t, SB_HEADS * SB_HEAD_DIM), BF16),
        scratch_shapes=[
            pltpu.VMEM((ATTN_GROUP, SB_HEAD_DIM, ATTN_TQ), F32),
            pltpu.VMEM((ATTN_GROUP, 1, ATTN_TQ), F32),
        ],
        compiler_params=_params(("parallel", "parallel", "arbitrary")),
        name="stickbreak_attn",
    )(qkvg, qkvg, qkvg, qkvg)


EVEN_TN = 1024
EVEN_ALIGNED = 2 * D_INNER + BC_DIM
EVEN_AFTER_DT = EVEN_ALIGNED + SSD_HEADS


def _even_layer(h, norm_w, ev_w_in_t, ev_w_out, layer, conv_w, conv_b, dt_bias, a_log,
                d_skip, ssd_norm_w, sgu_ln_w, sgu_ln_b, sgu_w, sgu_b, *, batch, seq, in_place):
    tiles = lambda cols: cols // EVEN_TN
    proj, dt_raw = _in_proj(
        h, norm_w, ev_w_in_t, layer, tm=1024, tn=EVEN_TN, transposed=True,
        segments=[(tiles(D_INNER), 0, 0),
                  (tiles(D_INNER + BC_DIM), D_INNER, tiles(4 * D_INNER)),
                  (tiles(3 * D_INNER), EVEN_AFTER_DT, tiles(D_INNER))],
        dt_start=EVEN_ALIGNED)
    y_a = _ssd_mixer(proj, dt_raw, conv_w, conv_b, dt_bias, a_log, d_skip, ssd_norm_w,
                     batch=batch, seq=seq)
    y_b = _sgu_mixer(proj, sgu_ln_w, sgu_ln_b, sgu_w, sgu_b)
    return _out_proj([y_a, y_b], [(ev_w_out, layer, 0), (ev_w_out, layer, 1)], h, None,
                     tm=512, tn=1024, in_place=in_place)


def _odd_layer(h, norm_w, od_w_in, od_w_out, layer, final_w, *, batch, seq):
    scale = SB_HEAD_DIM ** -0.5 * math.log2(math.e)
    tn = 1024
    qkvg = _in_proj(h, norm_w, od_w_in, layer, tm=1024, tn=tn,
                    segments=[(4 * D_INNER // tn, 0, 0)], scaled_tiles=D_INNER // tn, scale=scale)
    y = _attention(qkvg, batch=batch, seq=seq)
    tm, tn = (1024, 1024) if final_w is None else (512, D_MODEL)
    return _out_proj([y], [(od_w_out, layer, 0)], h, final_w, tm=tm, tn=tn, in_place=True)


def kernel(x, norm_w, final_norm_w, ev_w_in, ev_conv_w, ev_conv_b, ev_dt_bias, ev_a_log,
           ev_d_skip, ev_ssd_norm_w, ev_sgu_ln_w, ev_sgu_ln_b, ev_sgu_w, ev_sgu_b, ev_w_out,
           od_w_in, od_w_out):
    batch, seq, d = x.shape
    depth = norm_w.shape[0]
    h = x.reshape(batch * seq, d)
    ev_w_in_t = jnp.swapaxes(ev_w_in, 1, 2)
    for layer in range(depth):
        i = layer // 2
        if layer % 2 == 0:
            h = _even_layer(h, norm_w[layer], ev_w_in_t, ev_w_out, i, ev_conv_w[i],
                            ev_conv_b[i], ev_dt_bias[i], ev_a_log[i], ev_d_skip[i],
                            ev_ssd_norm_w[i], ev_sgu_ln_w[i], ev_sgu_ln_b[i], ev_sgu_w[i],
                            ev_sgu_b[i], batch=batch, seq=seq, in_place=layer > 0)
        else:
            fw = final_norm_w if layer == depth - 1 else None
            h = _odd_layer(h, norm_w[layer], od_w_in, od_w_out, i, fw, batch=batch, seq=seq)
    if depth % 2 == 1:
        raise NotImplementedError("final RMSNorm is fused into an odd last layer")
    return h.reshape(batch, seq, d)
```

```python
import functools
import math

import jax
import jax.numpy as jnp
from jax import lax
from jax.experimental import pallas as pl
from jax.experimental.pallas import tpu as pltpu

F32 = jnp.float32
BF16 = jnp.bfloat16

D_MODEL = 2048
NORM_EPS = 1e-5
SSD_HEAD_DIM = 64
SSD_HEADS = 32
SSD_GROUPS = 4
SSD_STATE = 128
CHUNK = 128
CONV_WIDTH = 4
D_INNER = 2048
BC_DIM = 2 * SSD_GROUPS * SSD_STATE
GROUP_COLS = D_INNER // SSD_GROUPS
SGU_GROUPS = 16
SGU_GROUP_DIM = 128
SB_HEADS = 16
SB_HEAD_DIM = 128
ATTN_TQ = 256
ATTN_TK = 128
ATTN_GROUP = 16
ATTN_SKIP_LOG2 = 160.0
DT_LANES = 128
DT_COPIES = 3

V7X_VMEM_LIMIT = 52 * 1024 * 1024
V7X_F32_SUBLANES = 8
CONV_HALO = V7X_F32_SUBLANES

IN_PROJ_TILE = (1024, 1024)
EVEN_OUT_TILE = (512, 1024)
ODD_OUT_TILE = (1024, 1024)
FINAL_OUT_TILE = (512, D_MODEL)


def _params(sem):
    return pltpu.CompilerParams(dimension_semantics=sem, vmem_limit_bytes=V7X_VMEM_LIMIT)


def _sigmoid(x):
    return 1.0 / (1.0 + jnp.exp(-x))


def _silu(x):
    return x * _sigmoid(x)


def _gelu_tanh(x):
    c, a = math.sqrt(2.0 / math.pi), 0.044715
    half = 0.5 * x
    return half + half * jnp.tanh(x * (c + (c * a) * (x * x)))


def _split3(x):
    h1 = x.astype(BF16)
    r1 = x - h1.astype(F32)
    h2 = r1.astype(BF16)
    r2 = r1 - h2.astype(F32)
    return h1, h2, r2.astype(BF16)


def _in_proj_kernel(*refs, transposed, scaled_tiles, scale, with_dt):
    x_ref, nw_ref, w_ref = refs[:3]
    hn_ref = refs[-1]
    if with_dt:
        wdt_ref, o_ref, dt_ref = refs[3:6]
    else:
        o_ref = refs[3]
    j = pl.program_id(1)

    def matmul(w_ref):
        w = w_ref[...].astype(BF16)
        if transposed:
            return lax.dot_general(hn_ref[...], w, (((1,), (1,)), ((), ())),
                                   preferred_element_type=F32)
        return jnp.dot(hn_ref[...], w, preferred_element_type=F32)

    @pl.when(j == 0)
    def _():
        x = x_ref[...]
        ms = jnp.mean(x * x, axis=-1, keepdims=True)
        hn_ref[...] = (x * lax.rsqrt(ms + NORM_EPS) * nw_ref[...]).astype(BF16)
        if with_dt:
            dt_ref[...] = matmul(wdt_ref)

    res = matmul(w_ref)
    if scaled_tiles:
        res = res * jnp.where(j < scaled_tiles, scale, 1.0)
    o_ref[...] = res.astype(o_ref.dtype)


def _in_proj(h, norm_w, w, layer, *, tm, tn, segments, transposed=False, scaled_tiles=0,
             scale=1.0, dt_start=None):
    m, d = h.shape
    n_total = sum(cnt for cnt, _, _ in segments)

    def pick(j, field, unit=1):
        out, first = 0, 0
        for seg in segments:
            inside = jnp.logical_and(j >= first, j < first + seg[0])
            step = tn // unit if field == 1 else 1
            out = out + jnp.where(inside, (j - first) * step + seg[field] // (unit if field == 1 else 1), 0)
            first += seg[0]
        return out

    if transposed:
        n_cols = w.shape[1]
        w = w.reshape(-1, d)
        sub = V7X_F32_SUBLANES
        assert n_cols % sub == 0 and all(first % sub == 0 for _, first, _ in segments)
        w_spec = pl.BlockSpec((pl.Element(tn), pl.Element(d)),
                              lambda i, j: (sub * (layer * n_cols // sub + pick(j, 1, sub)), 0))
    else:
        assert all(first % tn == 0 for _, first, _ in segments)
        w_spec = pl.BlockSpec((None, d, tn), lambda i, j: (layer, 0, pick(j, 1, tn)))
    in_specs = [pl.BlockSpec((tm, d), lambda i, j: (i, 0)),
                pl.BlockSpec((1, d), lambda i, j: (0, 0)),
                w_spec]
    args = [h, norm_w.reshape(1, d), w]
    out_specs = pl.BlockSpec((tm, tn), lambda i, j: (i, pick(j, 2)))
    out_shape = jax.ShapeDtypeStruct((m, n_total * tn), BF16)
    if dt_start is not None:
        assert transposed
        in_specs.append(pl.BlockSpec((pl.Element(DT_LANES), pl.Element(d)),
                                     lambda i, j: (sub * ((layer * n_cols + dt_start) // sub), 0)))
        args.append(w)
        out_specs = [out_specs, pl.BlockSpec((tm, DT_LANES), lambda i, j: (i, 0))]
        out_shape = [out_shape, jax.ShapeDtypeStruct((m, DT_LANES), F32)]
    return pl.pallas_call(
        functools.partial(_in_proj_kernel, transposed=transposed, scaled_tiles=scaled_tiles,
                          scale=scale, with_dt=dt_start is not None),
        grid=(m // tm, n_total),
        in_specs=in_specs,
        out_specs=out_specs,
        out_shape=out_shape,
        scratch_shapes=[pltpu.VMEM((tm, d), BF16)],
        compiler_params=_params(("parallel", "arbitrary")),
        name="norm_in_proj",
    )(*args)


def _out_proj_kernel(*refs, n_in, final_norm):
    y_refs = refs[:n_in]
    w_refs = refs[n_in:2 * n_in]
    h_ref = refs[2 * n_in]
    o_ref, wb_ref = refs[-2:]

    @pl.when(pl.program_id(1) == 0)
    def _():
        for k, w_ref in enumerate(w_refs):
            wb_ref[k] = w_ref[...].astype(BF16)

    acc = h_ref[...]
    for k, y_ref in enumerate(y_refs):
        acc = acc + jnp.dot(y_ref[...], wb_ref[k], preferred_element_type=F32)
    if final_norm:
        fw_ref = refs[2 * n_in + 1]
        ms = jnp.mean(acc * acc, axis=-1, keepdims=True)
        acc = acc * lax.rsqrt(ms + NORM_EPS) * fw_ref[...]
    o_ref[...] = acc


def _out_proj(ys, ws, h, final_w, *, tm, tn, in_place):
    m, n = h.shape
    n_in = len(ys)
    kdim = ys[0].shape[1]
    assert all(y.shape[1] == kdim for y in ys)
    final_norm = final_w is not None
    if final_norm:
        assert tn == n

    def w_spec(layer, row_block):
        return pl.BlockSpec((None, kdim, tn), lambda j, i: (layer, row_block, j),
                            pipeline_mode=pl.Buffered(1))

    in_specs = [pl.BlockSpec((tm, kdim), lambda j, i: (i, 0)) for _ in ys]
    in_specs += [w_spec(layer, rb) for _, layer, rb in ws]
    in_specs += [pl.BlockSpec((tm, tn), lambda j, i: (i, j))]
    args = list(ys) + [w for w, _, _ in ws] + [h]
    if final_norm:
        in_specs += [pl.BlockSpec((1, n), lambda j, i: (0, 0))]
        args += [final_w.reshape(1, n)]
    return pl.pallas_call(
        functools.partial(_out_proj_kernel, n_in=n_in, final_norm=final_norm),
        grid=(n // tn, m // tm),
        in_specs=in_specs,
        out_specs=pl.BlockSpec((tm, tn), lambda j, i: (i, j)),
        out_shape=jax.ShapeDtypeStruct((m, n), F32),
        scratch_shapes=[pltpu.VMEM((n_in, kdim, tn), BF16)],
        input_output_aliases={2 * n_in: 0} if in_place else {},
        compiler_params=_params(("arbitrary", "arbitrary")),
        name="out_proj",
    )(*args)


def _causal_conv(ext_ref, tail_ref, cur, w_ref, b_ref):
    ext_ref[0:CONV_HALO, :] = tail_ref[...]
    ext_ref[CONV_HALO:CONV_HALO + CHUNK, :] = cur
    tail_ref[...] = cur[CHUNK - CONV_HALO:, :]
    acc = b_ref[...] + w_ref[CONV_WIDTH - 1:CONV_WIDTH, :] * cur
    for k in range(CONV_WIDTH - 1):
        d = CONV_WIDTH - 1 - k
        acc = acc + w_ref[k:k + 1, :] * ext_ref[CONV_HALO - d:CONV_HALO - d + CHUNK, :]
    return acc


def _ssd_kernel(z_ref, x_ref, bc_ref, dt_ref, cwx_ref, cbx_ref, cwbc_ref, cbbc_ref,
                dtb_ref, alog_ref, dskip_ref, nw_ref, e64_ref, e128_ref,
                o_ref,
                state_ref, xtail_ref, bctail_ref, xext_ref, bcext_ref, y_ref):
    c = pl.program_id(1)

    @pl.when(c == 0)
    def _():
        state_ref[...] = jnp.zeros_like(state_ref)
        xtail_ref[...] = jnp.zeros_like(xtail_ref)
        bctail_ref[...] = jnp.zeros_like(bctail_ref)

    L = CHUNK
    row = lax.broadcasted_iota(jnp.int32, (L, L), 0)
    col = lax.broadcasted_iota(jnp.int32, (L, L), 1)
    causal = row >= col
    tri = jnp.where(causal, 1.0, 0.0).astype(BF16)
    head_lane = lax.broadcasted_iota(jnp.int32, (L, DT_LANES), 1) < SSD_HEADS

    dt_in = dt_ref[...] + dtb_ref[...]
    dt = jnp.where(head_lane, jnp.maximum(dt_in, 0.0) + jnp.log1p(jnp.exp(-jnp.abs(dt_in))), 0.0)
    a_dt = dt * (-jnp.exp(alog_ref[...]))
    p1, p2, p3 = _split3(a_dt)
    cs = (jnp.dot(tri, p1, preferred_element_type=F32)
          + jnp.dot(tri, p2, preferred_element_type=F32)
          + jnp.dot(tri, p3, preferred_element_type=F32))
    cs_t = cs.T

    def lane_pieces(v):
        q1, q2, q3 = (q.astype(F32) for q in _split3(v))
        return (q1 + pltpu.roll(q2, SSD_HEADS, 1) + pltpu.roll(q3, 2 * SSD_HEADS, 1)).astype(BF16)

    dt_p = lane_pieces(dt)
    cs_p = lane_pieces(cs)
    e64 = e64_ref[...]
    dt_e = jnp.dot(dt_p, e64, preferred_element_type=F32)
    cs_e = jnp.dot(cs_p, e64, preferred_element_type=F32)
    cs_col = jnp.dot(cs_p, e128_ref[...], preferred_element_type=F32)

    xs = _silu(_causal_conv(xext_ref, xtail_ref, x_ref[...].astype(F32), cwx_ref, cbx_ref))
    bc = _silu(_causal_conv(bcext_ref, bctail_ref, bc_ref[...].astype(F32), cwbc_ref, cbbc_ref))

    cs_last = cs_e[L - 1:L, :]
    x_dt = xs * dt_e
    x_dt_b = x_dt.astype(BF16)
    x_end_b = (x_dt * jnp.exp(cs_last - cs_e)).astype(BF16)
    in_decay = jnp.exp(cs_e)
    chunk_decay = jnp.exp(cs_last)

    lane_l = lax.broadcasted_iota(jnp.int32, (L, 2 * SSD_HEAD_DIM), 1)
    first_head = lane_l < SSD_HEAD_DIM
    ssq = jnp.zeros((L, 1), F32)
    for g in range(SSD_GROUPS):
        gs = slice(g * GROUP_COLS, (g + 1) * GROUP_COLS)
        b_g = bc[:, g * SSD_STATE:(g + 1) * SSD_STATE].astype(BF16)
        c_g = bc[:, (SSD_GROUPS + g) * SSD_STATE:(SSD_GROUPS + g + 1) * SSD_STATE].astype(BF16)
        scores = lax.dot_general(c_g, b_g, (((1,), (1,)), ((), ())),
                                 preferred_element_type=F32)
        heads_per_group = SSD_HEADS // SSD_GROUPS
        y_parts = []
        for pair in range(heads_per_group // 2):
            ms = []
            for r in range(2):
                h = g * heads_per_group + 2 * pair + r
                seg = cs_col[:, h * L:(h + 1) * L] - cs_t[h:h + 1, :]
                decay = jnp.exp(jnp.where(causal, seg, -jnp.inf))
                ms.append((scores * decay).astype(BF16))
            lhs = jnp.concatenate(ms, axis=1)
            c0 = g * GROUP_COLS + pair * 2 * SSD_HEAD_DIM
            xp = x_dt_b[:, c0:c0 + 2 * SSD_HEAD_DIM]
            zero = jnp.zeros_like(xp)
            rhs = jnp.concatenate([jnp.where(first_head, xp, zero),
                                   jnp.where(first_head, zero, xp)], axis=0)
            y_parts.append(jnp.dot(lhs, rhs, preferred_element_type=F32))
        y_diag = jnp.concatenate(y_parts, axis=1)

        st = state_ref[:, gs]
        y_off = jnp.dot(c_g, st.astype(BF16), preferred_element_type=F32) * in_decay[:, gs]
        new_st = lax.dot_general(b_g, x_end_b[:, gs], (((0,), (0,)), ((), ())),
                                 preferred_element_type=F32)
        state_ref[:, gs] = chunk_decay[:, gs] * st + new_st

        y = y_diag + y_off + xs[:, gs] * dskip_ref[:, gs]
        y = y * _silu(z_ref[:, gs].astype(F32))
        ssq = ssq + jnp.sum(y * y, axis=-1, keepdims=True)
        y_ref[:, gs] = y

    inv = lax.rsqrt(ssq * (1.0 / D_INNER) + NORM_EPS)
    o_ref[...] = (y_ref[...] * inv * nw_ref[...]).astype(o_ref.dtype)


def _ssd_mixer(proj, dt_raw, conv_w, conv_b, dt_bias, a_log, d_skip, norm_w, *, batch, seq):
    nc = seq // CHUNK
    t = batch * seq
    lanes = jnp.arange(DT_LANES)
    valid = lanes < DT_COPIES * SSD_HEADS
    head_of_lane = lanes % SSD_HEADS

    def tile_heads(v):
        return jnp.pad(v.astype(F32), (0, DT_LANES - SSD_HEADS)).reshape(1, DT_LANES)

    def expand(width):
        cols = jnp.arange(SSD_HEADS * width) // width
        return ((head_of_lane[:, None] == cols[None, :]) & valid[:, None]).astype(BF16)

    e64 = expand(SSD_HEAD_DIM)
    e128 = expand(CHUNK)
    d_skip_e = jnp.repeat(d_skip, SSD_HEAD_DIM).reshape(1, D_INNER)

    tok = lambda b, c: b * nc + c
    const = lambda b, c: (0, 0)
    full = lambda a: pl.BlockSpec(a.shape, const)
    cwx, cwbc = conv_w[:, :D_INNER], conv_w[:, D_INNER:]
    cbx, cbbc = conv_b[:D_INNER].reshape(1, -1), conv_b[D_INNER:].reshape(1, -1)
    dtb, alog = tile_heads(dt_bias), tile_heads(a_log)
    nw = norm_w.reshape(1, D_INNER)
    params = [cwx, cbx, cwbc, cbbc, dtb, alog, d_skip_e, nw, e64, e128]
    return pl.pallas_call(
        _ssd_kernel,
        grid=(batch, nc),
        in_specs=[
            pl.BlockSpec((CHUNK, D_INNER), lambda b, c: (tok(b, c), 0)),
            pl.BlockSpec((CHUNK, D_INNER), lambda b, c: (tok(b, c), 4)),
            pl.BlockSpec((CHUNK, BC_DIM), lambda b, c: (tok(b, c), 10)),
            pl.BlockSpec((CHUNK, DT_LANES), lambda b, c: (tok(b, c), 0)),
        ] + [full(p) for p in params],
        out_specs=pl.BlockSpec((CHUNK, D_INNER), lambda b, c: (tok(b, c), 0)),
        out_shape=jax.ShapeDtypeStruct((t, D_INNER), BF16),
        scratch_shapes=[
            pltpu.VMEM((SSD_STATE, D_INNER), F32),
            pltpu.VMEM((CONV_HALO, D_INNER), F32),
            pltpu.VMEM((CONV_HALO, BC_DIM), F32),
            pltpu.VMEM((CONV_HALO + CHUNK, D_INNER), F32),
            pltpu.VMEM((CONV_HALO + CHUNK, BC_DIM), F32),
            pltpu.VMEM((CHUNK, D_INNER), F32),
        ],
        compiler_params=_params(("parallel", "arbitrary")),
        name="ssd_mixer",
    )(proj, proj, proj, dt_raw, *params)


def _sgu_kernel(g_ref, u_ref, v_ref, lnw_ref, lnb_ref, ws_ref, bias_ref, o_ref, wtri_ref):
    @pl.when(pl.program_id(0) == 0)
    def _():
        row = lax.broadcasted_iota(jnp.int32, (CHUNK, CHUNK), 0)
        col = lax.broadcasted_iota(jnp.int32, (CHUNK, CHUNK), 1)
        for g in range(SGU_GROUPS):
            wtri_ref[g] = jnp.where(row >= col, ws_ref[g], 0.0).astype(BF16)

    v = _gelu_tanh(v_ref[...].astype(F32))
    mu = jnp.mean(v, axis=-1, keepdims=True)
    vc = v - mu
    var = jnp.mean(vc * vc, axis=-1, keepdims=True)
    vn = (vc * lax.rsqrt(var + NORM_EPS) * lnw_ref[...] + lnb_ref[...]).astype(BF16)
    for g in range(SGU_GROUPS):
        gs = slice(g * SGU_GROUP_DIM, (g + 1) * SGU_GROUP_DIM)
        mixed = jnp.dot(wtri_ref[g], vn[:, gs], preferred_element_type=F32) + bias_ref[:, gs]
        u = _gelu_tanh(u_ref[:, gs].astype(F32))
        o_ref[:, gs] = (u * mixed * _silu(g_ref[:, gs].astype(F32))).astype(o_ref.dtype)


def _sgu_mixer(proj, ln_w, ln_b, w_s, b_s):
    t = proj.shape[0]
    bias = jnp.repeat(b_s.T, SGU_GROUP_DIM, axis=1)
    const2 = lambda i: (0, 0)
    return pl.pallas_call(
        _sgu_kernel,
        grid=(t // CHUNK,),
        in_specs=[
            pl.BlockSpec((CHUNK, D_INNER), lambda i: (i, 1)),
            pl.BlockSpec((CHUNK, D_INNER), lambda i: (i, 2)),
            pl.BlockSpec((CHUNK, D_INNER), lambda i: (i, 3)),
            pl.BlockSpec((1, D_INNER), const2),
            pl.BlockSpec((1, D_INNER), const2),
            pl.BlockSpec(w_s.shape, lambda i: (0, 0, 0)),
            pl.BlockSpec((CHUNK, D_INNER), const2),
        ],
        out_specs=pl.BlockSpec((CHUNK, D_INNER), lambda i: (i, 0)),
        out_shape=jax.ShapeDtypeStruct((t, D_INNER), BF16),
        scratch_shapes=[pltpu.VMEM((SGU_GROUPS, CHUNK, CHUNK), BF16)],
        compiler_params=_params(("arbitrary",)),
        name="sgu_mixer",
    )(proj, proj, proj, ln_w.reshape(1, -1), ln_b.reshape(1, -1), w_s, bias)


def _attn_kernel(q_ref, k_ref, v_ref, g_ref, o_ref, acc_ref, carry_ref):
    i = pl.program_id(2)
    tq, tk = ATTN_TQ, ATTN_TK
    srow = lax.broadcasted_iota(jnp.int32, (tk, tk), 0)
    scol = lax.broadcasted_iota(jnp.int32, (tk, tk), 1)
    later = jnp.where(scol > srow, 1.0, 0.0).astype(BF16)

    def softplus2(z):
        return jnp.maximum(z, 0.0) + jnp.log2(1.0 + jnp.exp2(-jnp.abs(z)))

    def later_sums(sp, nsub):
        hi = sp.astype(BF16)
        afts, tots = [], []
        for r in range(nsub):
            aft = jnp.dot(later, hi[r * tk:(r + 1) * tk], preferred_element_type=F32)
            afts.append(aft)
            tots.append(aft[0:1, :] + sp[r * tk:r * tk + 1, :])
        return afts, tots

    def block_weights(z, sp, afts, tots, carry, nsub):
        ws = [None] * nsub
        for r in reversed(range(nsub)):
            rs = slice(r * tk, (r + 1) * tk)
            ws[r] = jnp.exp2((z[rs] - sp[rs]) - afts[r] - carry).astype(BF16)
            carry = carry + tots[r]
        return jnp.concatenate(ws, axis=0), carry

    def values(ks, nkeys, w):
        return lax.dot_general(v_ref[pl.ds(ks, nkeys), :], w, (((0,), (0,)), ((), ())),
                               preferred_element_type=F32)

    tiles = range(ATTN_GROUP)
    nsub = tq // tk
    tile_idx = [i * ATTN_GROUP + g for g in tiles]
    diag_start = [pl.multiple_of(t * tq, tq) for t in tile_idx]
    prev_start = [pl.multiple_of(jnp.maximum(t - 1, 0) * tq, tq) for t in tile_idx]
    keep = [jnp.where(t > 0, 1.0, 0.0) for t in tile_idx]
    below = (lax.broadcasted_iota(jnp.int32, (tq, tq), 0)
             < lax.broadcasted_iota(jnp.int32, (tq, tq), 1))

    def scores(g, ks):
        return lax.dot_general(k_ref[pl.ds(ks, tq), :], q_ref[g * tq:(g + 1) * tq, :],
                               (((1,), (1,)), ((), ())), preferred_element_type=F32)

    z_diag = [scores(g, diag_start[g]) for g in tiles]
    z_prev = [scores(g, prev_start[g]) for g in tiles]
    sp_diag = [jnp.where(below, softplus2(z), 0.0) for z in z_diag]
    sp_prev = [softplus2(z_prev[g]) * keep[g] for g in tiles]
    sums_diag = [later_sums(sp, nsub) for sp in sp_diag]
    sums_prev = [later_sums(sp, nsub) for sp in sp_prev]
    for g in tiles:
        w, carry = block_weights(z_diag[g], sp_diag[g], *sums_diag[g], jnp.zeros((1, tq), F32), nsub)
        w = jnp.where(below, w, jnp.zeros_like(w))
        acc = values(diag_start[g], tq, w)
        w, carry = block_weights(z_prev[g], sp_prev[g], *sums_prev[g], carry, nsub)
        acc_ref[g] = acc + values(prev_start[g], tq, w * keep[g].astype(BF16))
        carry_ref[g] = carry

    for g in tiles:
        def more(state):
            j, least = state
            return jnp.logical_and(j >= 0, least < ATTN_SKIP_LOG2)

        def earlier(state, g=g):
            j, _ = state
            ks = pl.multiple_of(j * tq, tq)
            z = lax.dot_general(k_ref[pl.ds(ks, tq), :], q_ref[g * tq:(g + 1) * tq, :],
                                (((1,), (1,)), ((), ())), preferred_element_type=F32)
            sp = softplus2(z)
            afts, tots = later_sums(sp, tq // tk)
            w, carry = block_weights(z, sp, afts, tots, carry_ref[g], tq // tk)
            acc_ref[g] += values(ks, tq, w)
            carry_ref[g] = carry
            return j - 1, jnp.min(carry)

        lax.while_loop(more, earlier, (tile_idx[g] - 2, jnp.min(carry_ref[g])))

    for g in tiles:
        rows = slice(g * tq, (g + 1) * tq)
        o_ref[rows, :] = (acc_ref[g].T * _silu(g_ref[rows, :].astype(F32))).astype(o_ref.dtype)


def _attention(qkvg, *, batch, seq):
    rows = ATTN_GROUP * ATTN_TQ
    assert seq % rows == 0
    nq = seq // rows
    t = batch * seq
    return pl.pallas_call(
        _attn_kernel,
        grid=(batch, SB_HEADS, nq),
        in_specs=[
            pl.BlockSpec((rows, SB_HEAD_DIM), lambda b, h, i: (b * nq + i, h)),
            pl.BlockSpec((seq, SB_HEAD_DIM), lambda b, h, i: (b, SB_HEADS + h)),
            pl.BlockSpec((seq, SB_HEAD_DIM), lambda b, h, i: (b, 2 * SB_HEADS + h)),
            pl.BlockSpec((rows, SB_HEAD_DIM), lambda b, h, i: (b * nq + i, 3 * SB_HEADS + h)),
        ],
        out_specs=pl.BlockSpec((rows, SB_HEAD_DIM), lambda b, h, i: (b * nq + i, h)),
        out_shape=jax.ShapeDtypeStruct((t, SB_HEADS * SB_HEAD_DIM), BF16),
        scratch_shapes=[
            pltpu.VMEM((ATTN_GROUP, SB_HEAD_DIM, ATTN_TQ), F32),
            pltpu.VMEM((ATTN_GROUP, 1, ATTN_TQ), F32),
        ],
        compiler_params=_params(("parallel", "parallel", "arbitrary")),
        name="stickbreak_attn",
    )(qkvg, qkvg, qkvg, qkvg)


EVEN_ALIGNED = 2 * D_INNER + BC_DIM
EVEN_AFTER_DT = EVEN_ALIGNED + SSD_HEADS


def _even_layer(h, norm_w, ev_w_in_t, ev_w_out, layer, conv_w, conv_b, dt_bias, a_log,
                d_skip, ssd_norm_w, sgu_ln_w, sgu_ln_b, sgu_w, sgu_b, *, batch, seq, in_place):
    tm, tn = IN_PROJ_TILE
    tiles = lambda cols: cols // tn
    proj, dt_raw = _in_proj(
        h, norm_w, ev_w_in_t, layer, tm=tm, tn=tn, transposed=True,
        segments=[(tiles(D_INNER), 0, 0),
                  (tiles(D_INNER + BC_DIM), D_INNER, tiles(4 * D_INNER)),
                  (tiles(3 * D_INNER), EVEN_AFTER_DT, tiles(D_INNER))],
        dt_start=EVEN_ALIGNED)
    y_a = _ssd_mixer(proj, dt_raw, conv_w, conv_b, dt_bias, a_log, d_skip, ssd_norm_w,
                     batch=batch, seq=seq)
    y_b = _sgu_mixer(proj, sgu_ln_w, sgu_ln_b, sgu_w, sgu_b)
    return _out_proj([y_a, y_b], [(ev_w_out, layer, 0), (ev_w_out, layer, 1)], h, None,
                     tm=EVEN_OUT_TILE[0], tn=EVEN_OUT_TILE[1], in_place=in_place)


def _odd_layer(h, norm_w, od_w_in, od_w_out, layer, final_w, *, batch, seq):
    scale = SB_HEAD_DIM ** -0.5 * math.log2(math.e)
    tm, tn = IN_PROJ_TILE
    qkvg = _in_proj(h, norm_w, od_w_in, layer, tm=tm, tn=tn,
                    segments=[(4 * D_INNER // tn, 0, 0)], scaled_tiles=D_INNER // tn, scale=scale)
    y = _attention(qkvg, batch=batch, seq=seq)
    tm, tn = ODD_OUT_TILE if final_w is None else FINAL_OUT_TILE
    return _out_proj([y], [(od_w_out, layer, 0)], h, final_w, tm=tm, tn=tn, in_place=True)


def kernel(x, norm_w, final_norm_w, ev_w_in, ev_conv_w, ev_conv_b, ev_dt_bias, ev_a_log,
           ev_d_skip, ev_ssd_norm_w, ev_sgu_ln_w, ev_sgu_ln_b, ev_sgu_w, ev_sgu_b, ev_w_out,
           od_w_in, od_w_out):
    batch, seq, d = x.shape
    depth = norm_w.shape[0]
    h = x.reshape(batch * seq, d)
    ev_w_in_t = jnp.swapaxes(ev_w_in, 1, 2)
    for layer in range(depth):
        i = layer // 2
        if layer % 2 == 0:
            h = _even_layer(h, norm_w[layer], ev_w_in_t, ev_w_out, i, ev_conv_w[i],
                            ev_conv_b[i], ev_dt_bias[i], ev_a_log[i], ev_d_skip[i],
                            ev_ssd_norm_w[i], ev_sgu_ln_w[i], ev_sgu_ln_b[i], ev_sgu_w[i],
                            ev_sgu_b[i], batch=batch, seq=seq, in_place=layer > 0)
        else:
            fw = final_norm_w if layer == depth - 1 else None
            h = _odd_layer(h, norm_w[layer], od_w_in, od_w_out, i, fw, batch=batch, seq=seq)
    if depth % 2 == 1:
        raise NotImplementedError("final RMSNorm is fused into an odd last layer")
    return h.reshape(batch, seq, d)
```

```python
import functools
import math

import jax
import jax.numpy as jnp
from jax import lax
from jax.experimental import pallas as pl
from jax.experimental.pallas import tpu as pltpu

F32 = jnp.float32
BF16 = jnp.bfloat16

D_MODEL = 2048
NORM_EPS = 1e-5
SSD_HEAD_DIM = 64
SSD_HEADS = 32
SSD_GROUPS = 4
SSD_STATE = 128
CHUNK = 128
CONV_WIDTH = 4
D_INNER = 2048
BC_DIM = 2 * SSD_GROUPS * SSD_STATE
GROUP_COLS = D_INNER // SSD_GROUPS
SGU_GROUPS = 16
SGU_GROUP_DIM = 128
SB_HEADS = 16
SB_HEAD_DIM = 128
ATTN_TQ = 256
ATTN_TK = 128
ATTN_GROUP = 16
ATTN_SKIP_LOG2 = 160.0
DT_LANES = 128
DT_COPIES = 3

V7X_VMEM_LIMIT = 52 * 1024 * 1024
V7X_F32_SUBLANES = 8
CONV_HALO = V7X_F32_SUBLANES

IN_PROJ_TILE = (1024, 1024)
EVEN_OUT_TILE = (512, 1024)
ODD_OUT_TILE = (1024, 1024)
FINAL_OUT_TILE = (512, D_MODEL)


def _params(sem):
    return pltpu.CompilerParams(dimension_semantics=sem, vmem_limit_bytes=V7X_VMEM_LIMIT)


def _sigmoid(x):
    return 1.0 / (1.0 + jnp.exp(-x))


def _silu(x):
    return x * _sigmoid(x)


def _gelu_tanh(x):
    c, a = math.sqrt(2.0 / math.pi), 0.044715
    half = 0.5 * x
    return half + half * jnp.tanh(x * (c + (c * a) * (x * x)))


def _split3(x):
    h1 = x.astype(BF16)
    r1 = x - h1.astype(F32)
    h2 = r1.astype(BF16)
    r2 = r1 - h2.astype(F32)
    return h1, h2, r2.astype(BF16)


def _in_proj_kernel(*refs, transposed, scaled_tiles, scale, with_dt):
    x_ref, nw_ref, w_ref = refs[:3]
    hn_ref = refs[-1]
    if with_dt:
        wdt_ref, o_ref, dt_ref = refs[3:6]
    else:
        o_ref = refs[3]
    j = pl.program_id(1)

    def matmul(w_ref):
        w = w_ref[...].astype(BF16)
        if transposed:
            return lax.dot_general(hn_ref[...], w, (((1,), (1,)), ((), ())),
                                   preferred_element_type=F32)
        return jnp.dot(hn_ref[...], w, preferred_element_type=F32)

    @pl.when(j == 0)
    def _():
        x = x_ref[...]
        ms = jnp.mean(x * x, axis=-1, keepdims=True)
        hn_ref[...] = (x * lax.rsqrt(ms + NORM_EPS) * nw_ref[...]).astype(BF16)
        if with_dt:
            dt_ref[...] = matmul(wdt_ref)

    res = matmul(w_ref)
    if scaled_tiles:
        res = res * jnp.where(j < scaled_tiles, scale, 1.0)
    o_ref[...] = res.astype(o_ref.dtype)


def _in_proj(h, norm_w, w, layer, *, tm, tn, segments, transposed=False, scaled_tiles=0,
             scale=1.0, dt_start=None):
    m, d = h.shape
    n_total = sum(cnt for cnt, _, _ in segments)

    def pick(j, field, unit=1):
        out, first = 0, 0
        for seg in segments:
            inside = jnp.logical_and(j >= first, j < first + seg[0])
            step = tn // unit if field == 1 else 1
            out = out + jnp.where(inside, (j - first) * step + seg[field] // (unit if field == 1 else 1), 0)
            first += seg[0]
        return out

    if transposed:
        n_cols = w.shape[1]
        w = w.reshape(-1, d)
        sub = V7X_F32_SUBLANES
        assert n_cols % sub == 0 and all(first % sub == 0 for _, first, _ in segments)
        w_spec = pl.BlockSpec((pl.Element(tn), pl.Element(d)),
                              lambda i, j: (sub * (layer * n_cols // sub + pick(j, 1, sub)), 0))
    else:
        assert all(first % tn == 0 for _, first, _ in segments)
        w_spec = pl.BlockSpec((None, d, tn), lambda i, j: (layer, 0, pick(j, 1, tn)))
    in_specs = [pl.BlockSpec((tm, d), lambda i, j: (i, 0)),
                pl.BlockSpec((1, d), lambda i, j: (0, 0)),
                w_spec]
    args = [h, norm_w.reshape(1, d), w]
    out_specs = pl.BlockSpec((tm, tn), lambda i, j: (i, pick(j, 2)))
    out_shape = jax.ShapeDtypeStruct((m, n_total * tn), BF16)
    if dt_start is not None:
        assert transposed
        in_specs.append(pl.BlockSpec((pl.Element(DT_LANES), pl.Element(d)),
                                     lambda i, j: (sub * ((layer * n_cols + dt_start) // sub), 0)))
        args.append(w)
        out_specs = [out_specs, pl.BlockSpec((tm, DT_LANES), lambda i, j: (i, 0))]
        out_shape = [out_shape, jax.ShapeDtypeStruct((m, DT_LANES), F32)]
    return pl.pallas_call(
        functools.partial(_in_proj_kernel, transposed=transposed, scaled_tiles=scaled_tiles,
                          scale=scale, with_dt=dt_start is not None),
        grid=(m // tm, n_total),
        in_specs=in_specs,
        out_specs=out_specs,
        out_shape=out_shape,
        scratch_shapes=[pltpu.VMEM((tm, d), BF16)],
        compiler_params=_params(("parallel", "arbitrary")),
        name="norm_in_proj",
    )(*args)


def _out_proj_kernel(*refs, n_in, final_norm):
    y_refs = refs[:n_in]
    w_refs = refs[n_in:2 * n_in]
    h_ref = refs[2 * n_in]
    o_ref, wb_ref = refs[-2:]

    @pl.when(pl.program_id(1) == 0)
    def _():
        for k, w_ref in enumerate(w_refs):
            wb_ref[k] = w_ref[...].astype(BF16)

    acc = h_ref[...]
    for k, y_ref in enumerate(y_refs):
        acc = acc + jnp.dot(y_ref[...], wb_ref[k], preferred_element_type=F32)
    if final_norm:
        fw_ref = refs[2 * n_in + 1]
        ms = jnp.mean(acc * acc, axis=-1, keepdims=True)
        acc = acc * lax.rsqrt(ms + NORM_EPS) * fw_ref[...]
    o_ref[...] = acc


def _out_proj(ys, ws, h, final_w, *, tm, tn, in_place):
    m, n = h.shape
    n_in = len(ys)
    kdim = ys[0].shape[1]
    assert all(y.shape[1] == kdim for y in ys)
    final_norm = final_w is not None
    if final_norm:
        assert tn == n

    def w_spec(layer, row_block):
        return pl.BlockSpec((None, kdim, tn), lambda j, i: (layer, row_block, j),
                            pipeline_mode=pl.Buffered(1))

    in_specs = [pl.BlockSpec((tm, kdim), lambda j, i: (i, 0)) for _ in ys]
    in_specs += [w_spec(layer, rb) for _, layer, rb in ws]
    in_specs += [pl.BlockSpec((tm, tn), lambda j, i: (i, j))]
    args = list(ys) + [w for w, _, _ in ws] + [h]
    if final_norm:
        in_specs += [pl.BlockSpec((1, n), lambda j, i: (0, 0))]
        args += [final_w.reshape(1, n)]
    return pl.pallas_call(
        functools.partial(_out_proj_kernel, n_in=n_in, final_norm=final_norm),
        grid=(n // tn, m // tm),
        in_specs=in_specs,
        out_specs=pl.BlockSpec((tm, tn), lambda j, i: (i, j)),
        out_shape=jax.ShapeDtypeStruct((m, n), F32),
        scratch_shapes=[pltpu.VMEM((n_in, kdim, tn), BF16)],
        input_output_aliases={2 * n_in: 0} if in_place else {},
        compiler_params=_params(("arbitrary", "arbitrary")),
        name="out_proj",
    )(*args)


def _causal_conv(ext_ref, tail_ref, cur, w_ref, b_ref):
    ext_ref[0:CONV_HALO, :] = tail_ref[...]
    ext_ref[CONV_HALO:CONV_HALO + CHUNK, :] = cur
    tail_ref[...] = cur[CHUNK - CONV_HALO:, :]
    acc = b_ref[...] + w_ref[CONV_WIDTH - 1:CONV_WIDTH, :] * cur
    for k in range(CONV_WIDTH - 1):
        d = CONV_WIDTH - 1 - k
        acc = acc + w_ref[k:k + 1, :] * ext_ref[CONV_HALO - d:CONV_HALO - d + CHUNK, :]
    return acc


def _ssd_kernel(z_ref, x_ref, bc_ref, dt_ref, cwx_ref, cbx_ref, cwbc_ref, cbbc_ref,
                dtb_ref, alog_ref, dskip_ref, nw_ref, e64_ref, e128_ref,
                o_ref,
                state_ref, xtail_ref, bctail_ref, xext_ref, bcext_ref, y_ref):
    c = pl.program_id(1)

    @pl.when(c == 0)
    def _():
        state_ref[...] = jnp.zeros_like(state_ref)
        xtail_ref[...] = jnp.zeros_like(xtail_ref)
        bctail_ref[...] = jnp.zeros_like(bctail_ref)

    L = CHUNK
    row = lax.broadcasted_iota(jnp.int32, (L, L), 0)
    col = lax.broadcasted_iota(jnp.int32, (L, L), 1)
    causal = row >= col
    tri = jnp.where(causal, 1.0, 0.0).astype(BF16)
    head_lane = lax.broadcasted_iota(jnp.int32, (L, DT_LANES), 1) < SSD_HEADS

    dt_in = dt_ref[...] + dtb_ref[...]
    dt = jnp.where(head_lane, jnp.maximum(dt_in, 0.0) + jnp.log1p(jnp.exp(-jnp.abs(dt_in))), 0.0)
    a_dt = dt * (-jnp.exp(alog_ref[...]))
    p1, p2, p3 = _split3(a_dt)
    cs = (jnp.dot(tri, p1, preferred_element_type=F32)
          + jnp.dot(tri, p2, preferred_element_type=F32)
          + jnp.dot(tri, p3, preferred_element_type=F32))
    cs_t = cs.T

    def lane_pieces(v):
        q1, q2, q3 = (q.astype(F32) for q in _split3(v))
        return (q1 + pltpu.roll(q2, SSD_HEADS, 1) + pltpu.roll(q3, 2 * SSD_HEADS, 1)).astype(BF16)

    dt_p = lane_pieces(dt)
    cs_p = lane_pieces(cs)
    e64 = e64_ref[...]
    dt_e = jnp.dot(dt_p, e64, preferred_element_type=F32)
    cs_e = jnp.dot(cs_p, e64, preferred_element_type=F32)
    cs_col = jnp.dot(cs_p, e128_ref[...], preferred_element_type=F32)

    xs = _silu(_causal_conv(xext_ref, xtail_ref, x_ref[...].astype(F32), cwx_ref, cbx_ref))
    bc = _silu(_causal_conv(bcext_ref, bctail_ref, bc_ref[...].astype(F32), cwbc_ref, cbbc_ref))

    cs_last = cs_e[L - 1:L, :]
    x_dt = xs * dt_e
    x_dt_b = x_dt.astype(BF16)
    x_end_b = (x_dt * jnp.exp(cs_last - cs_e)).astype(BF16)
    in_decay = jnp.exp(cs_e)
    chunk_decay = jnp.exp(cs_last)

    lane_l = lax.broadcasted_iota(jnp.int32, (L, 2 * SSD_HEAD_DIM), 1)
    first_head = lane_l < SSD_HEAD_DIM
    ssq = jnp.zeros((L, 1), F32)
    for g in range(SSD_GROUPS):
        gs = slice(g * GROUP_COLS, (g + 1) * GROUP_COLS)
        b_g = bc[:, g * SSD_STATE:(g + 1) * SSD_STATE].astype(BF16)
        c_g = bc[:, (SSD_GROUPS + g) * SSD_STATE:(SSD_GROUPS + g + 1) * SSD_STATE].astype(BF16)
        scores = lax.dot_general(c_g, b_g, (((1,), (1,)), ((), ())),
                                 preferred_element_type=F32)
        heads_per_group = SSD_HEADS // SSD_GROUPS
        y_parts = []
        for pair in range(heads_per_group // 2):
            ms = []
            for r in range(2):
                h = g * heads_per_group + 2 * pair + r
                seg = cs_col[:, h * L:(h + 1) * L] - cs_t[h:h + 1, :]
                decay = jnp.exp(jnp.where(causal, seg, -jnp.inf))
                ms.append((scores * decay).astype(BF16))
            lhs = jnp.concatenate(ms, axis=1)
            c0 = g * GROUP_COLS + pair * 2 * SSD_HEAD_DIM
            xp = x_dt_b[:, c0:c0 + 2 * SSD_HEAD_DIM]
            zero = jnp.zeros_like(xp)
            rhs = jnp.concatenate([jnp.where(first_head, xp, zero),
                                   jnp.where(first_head, zero, xp)], axis=0)
            y_parts.append(jnp.dot(lhs, rhs, preferred_element_type=F32))
        y_diag = jnp.concatenate(y_parts, axis=1)

        st = state_ref[:, gs]
        y_off = jnp.dot(c_g, st.astype(BF16), preferred_element_type=F32) * in_decay[:, gs]
        new_st = lax.dot_general(b_g, x_end_b[:, gs], (((0,), (0,)), ((), ())),
                                 preferred_element_type=F32)
        state_ref[:, gs] = chunk_decay[:, gs] * st + new_st

        y = y_diag + y_off + xs[:, gs] * dskip_ref[:, gs]
        y = y * _silu(z_ref[:, gs].astype(F32))
        ssq = ssq + jnp.sum(y * y, axis=-1, keepdims=True)
        y_ref[:, gs] = y

    inv = lax.rsqrt(ssq * (1.0 / D_INNER) + NORM_EPS)
    o_ref[...] = (y_ref[...] * inv * nw_ref[...]).astype(o_ref.dtype)


def _ssd_mixer(proj, dt_raw, conv_w, conv_b, dt_bias, a_log, d_skip, norm_w, *, batch, seq):
    nc = seq // CHUNK
    t = batch * seq
    lanes = jnp.arange(DT_LANES)
    valid = lanes < DT_COPIES * SSD_HEADS
    head_of_lane = lanes % SSD_HEADS

    def tile_heads(v):
        return jnp.pad(v.astype(F32), (0, DT_LANES - SSD_HEADS)).reshape(1, DT_LANES)

    def expand(width):
        cols = jnp.arange(SSD_HEADS * width) // width
        return ((head_of_lane[:, None] == cols[None, :]) & valid[:, None]).astype(BF16)

    e64 = expand(SSD_HEAD_DIM)
    e128 = expand(CHUNK)
    d_skip_e = jnp.repeat(d_skip, SSD_HEAD_DIM).reshape(1, D_INNER)

    tok = lambda b, c: b * nc + c
    const = lambda b, c: (0, 0)
    full = lambda a: pl.BlockSpec(a.shape, const)
    cwx, cwbc = conv_w[:, :D_INNER], conv_w[:, D_INNER:]
    cbx, cbbc = conv_b[:D_INNER].reshape(1, -1), conv_b[D_INNER:].reshape(1, -1)
    dtb, alog = tile_heads(dt_bias), tile_heads(a_log)
    nw = norm_w.reshape(1, D_INNER)
    params = [cwx, cbx, cwbc, cbbc, dtb, alog, d_skip_e, nw, e64, e128]
    return pl.pallas_call(
        _ssd_kernel,
        grid=(batch, nc),
        in_specs=[
            pl.BlockSpec((CHUNK, D_INNER), lambda b, c: (tok(b, c), 0)),
            pl.BlockSpec((CHUNK, D_INNER), lambda b, c: (tok(b, c), 4)),
            pl.BlockSpec((CHUNK, BC_DIM), lambda b, c: (tok(b, c), 10)),
            pl.BlockSpec((CHUNK, DT_LANES), lambda b, c: (tok(b, c), 0)),
        ] + [full(p) for p in params],
        out_specs=pl.BlockSpec((CHUNK, D_INNER), lambda b, c: (tok(b, c), 0)),
        out_shape=jax.ShapeDtypeStruct((t, D_INNER), BF16),
        scratch_shapes=[
            pltpu.VMEM((SSD_STATE, D_INNER), F32),
            pltpu.VMEM((CONV_HALO, D_INNER), F32),
            pltpu.VMEM((CONV_HALO, BC_DIM), F32),
            pltpu.VMEM((CONV_HALO + CHUNK, D_INNER), F32),
            pltpu.VMEM((CONV_HALO + CHUNK, BC_DIM), F32),
            pltpu.VMEM((CHUNK, D_INNER), F32),
        ],
        compiler_params=_params(("parallel", "arbitrary")),
        name="ssd_mixer",
    )(proj, proj, proj, dt_raw, *params)


def _sgu_kernel(g_ref, u_ref, v_ref, lnw_ref, lnb_ref, ws_ref, bias_ref, o_ref, wtri_ref):
    @pl.when(pl.program_id(0) == 0)
    def _():
        row = lax.broadcasted_iota(jnp.int32, (CHUNK, CHUNK), 0)
        col = lax.broadcasted_iota(jnp.int32, (CHUNK, CHUNK), 1)
        for g in range(SGU_GROUPS):
            wtri_ref[g] = jnp.where(row >= col, ws_ref[g], 0.0).astype(BF16)

    v = _gelu_tanh(v_ref[...].astype(F32))
    mu = jnp.mean(v, axis=-1, keepdims=True)
    vc = v - mu
    var = jnp.mean(vc * vc, axis=-1, keepdims=True)
    vn = (vc * lax.rsqrt(var + NORM_EPS) * lnw_ref[...] + lnb_ref[...]).astype(BF16)
    for g in range(SGU_GROUPS):
        gs = slice(g * SGU_GROUP_DIM, (g + 1) * SGU_GROUP_DIM)
        mixed = jnp.dot(wtri_ref[g], vn[:, gs], preferred_element_type=F32) + bias_ref[:, gs]
        u = _gelu_tanh(u_ref[:, gs].astype(F32))
        o_ref[:, gs] = (u * mixed * _silu(g_ref[:, gs].astype(F32))).astype(o_ref.dtype)


def _sgu_mixer(proj, ln_w, ln_b, w_s, b_s):
    t = proj.shape[0]
    bias = jnp.repeat(b_s.T, SGU_GROUP_DIM, axis=1)
    const2 = lambda i: (0, 0)
    return pl.pallas_call(
        _sgu_kernel,
        grid=(t // CHUNK,),
        in_specs=[
            pl.BlockSpec((CHUNK, D_INNER), lambda i: (i, 1)),
            pl.BlockSpec((CHUNK, D_INNER), lambda i: (i, 2)),
            pl.BlockSpec((CHUNK, D_INNER), lambda i: (i, 3)),
            pl.BlockSpec((1, D_INNER), const2),
            pl.BlockSpec((1, D_INNER), const2),
            pl.BlockSpec(w_s.shape, lambda i: (0, 0, 0)),
            pl.BlockSpec((CHUNK, D_INNER), const2),
        ],
        out_specs=pl.BlockSpec((CHUNK, D_INNER), lambda i: (i, 0)),
        out_shape=jax.ShapeDtypeStruct((t, D_INNER), BF16),
        scratch_shapes=[pltpu.VMEM((SGU_GROUPS, CHUNK, CHUNK), BF16)],
        compiler_params=_params(("arbitrary",)),
        name="sgu_mixer",
    )(proj, proj, proj, ln_w.reshape(1, -1), ln_b.reshape(1, -1), w_s, bias)


def _attn_kernel(q_ref, k_ref, v_ref, g_ref, o_ref, acc_ref, carry_ref):
    i = pl.program_id(2)
    tq, tk = ATTN_TQ, ATTN_TK
    srow = lax.broadcasted_iota(jnp.int32, (tk, tk), 0)
    scol = lax.broadcasted_iota(jnp.int32, (tk, tk), 1)
    later = jnp.where(scol > srow, 1.0, 0.0).astype(BF16)

    def softplus2(z):
        return jnp.maximum(z, 0.0) + jnp.log2(1.0 + jnp.exp2(-jnp.abs(z)))

    def later_sums(sp, nsub):
        hi = sp.astype(BF16)
        afts, tots = [], []
        for r in range(nsub):
            aft = jnp.dot(later, hi[r * tk:(r + 1) * tk], preferred_element_type=F32)
            afts.append(aft)
            tots.append(aft[0:1, :] + sp[r * tk:r * tk + 1, :])
        return afts, tots

    def block_weights(z, sp, afts, tots, carry, nsub):
        ws = [None] * nsub
        for r in reversed(range(nsub)):
            rs = slice(r * tk, (r + 1) * tk)
            ws[r] = jnp.exp2((z[rs] - sp[rs]) - afts[r] - carry).astype(BF16)
            carry = carry + tots[r]
        return jnp.concatenate(ws, axis=0), carry

    def values(ks, nkeys, w):
        return lax.dot_general(v_ref[pl.ds(ks, nkeys), :], w, (((0,), (0,)), ((), ())),
                               preferred_element_type=F32)

    tiles = range(ATTN_GROUP)
    nsub = tq // tk
    tile_idx = [i * ATTN_GROUP + g for g in tiles]
    diag_start = [pl.multiple_of(t * tq, tq) for t in tile_idx]
    prev_start = [pl.multiple_of(jnp.maximum(t - 1, 0) * tq, tq) for t in tile_idx]
    keep = [jnp.where(t > 0, 1.0, 0.0) for t in tile_idx]
    below = (lax.broadcasted_iota(jnp.int32, (tq, tq), 0)
             < lax.broadcasted_iota(jnp.int32, (tq, tq), 1))

    def scores(g, ks):
        return lax.dot_general(k_ref[pl.ds(ks, tq), :], q_ref[g * tq:(g + 1) * tq, :],
                               (((1,), (1,)), ((), ())), preferred_element_type=F32)

    z_diag = [scores(g, diag_start[g]) for g in tiles]
    z_prev = [scores(g, prev_start[g]) for g in tiles]
    sp_diag = [jnp.where(below, softplus2(z), 0.0) for z in z_diag]
    sp_prev = [softplus2(z_prev[g]) * keep[g] for g in tiles]
    sums_diag = [later_sums(sp, nsub) for sp in sp_diag]
    sums_prev = [later_sums(sp, nsub) for sp in sp_prev]
    for g in tiles:
        w, carry = block_weights(z_diag[g], sp_diag[g], *sums_diag[g], jnp.zeros((1, tq), F32), nsub)
        w = jnp.where(below, w, jnp.zeros_like(w))
        acc = values(diag_start[g], tq, w)
        w, carry = block_weights(z_prev[g], sp_prev[g], *sums_prev[g], carry, nsub)
        w = jnp.where(tile_idx[g] > 0, w, jnp.zeros_like(w))
        acc_ref[g] = acc + values(prev_start[g], tq, w)
        carry_ref[g] = carry

    for g in tiles:
        def more(state):
            j, least = state
            return jnp.logical_and(j >= 0, least < ATTN_SKIP_LOG2)

        def earlier(state, g=g):
            j, _ = state
            ks = pl.multiple_of(j * tq, tq)
            z = lax.dot_general(k_ref[pl.ds(ks, tq), :], q_ref[g * tq:(g + 1) * tq, :],
                                (((1,), (1,)), ((), ())), preferred_element_type=F32)
            sp = softplus2(z)
            afts, tots = later_sums(sp, tq // tk)
            w, carry = block_weights(z, sp, afts, tots, carry_ref[g], tq // tk)
            acc_ref[g] += values(ks, tq, w)
            carry_ref[g] = carry
            return j - 1, jnp.min(carry)

        lax.while_loop(more, earlier, (tile_idx[g] - 2, jnp.min(carry_ref[g])))

    for g in tiles:
        rows = slice(g * tq, (g + 1) * tq)
        o_ref[rows, :] = (acc_ref[g].T * _silu(g_ref[rows, :].astype(F32))).astype(o_ref.dtype)


def _attention(qkvg, *, batch, seq):
    rows = ATTN_GROUP * ATTN_TQ
    assert seq % rows == 0
    nq = seq // rows
    t = batch * seq
    return pl.pallas_call(
        _attn_kernel,
        grid=(batch, SB_HEADS, nq),
        in_specs=[
            pl.BlockSpec((rows, SB_HEAD_DIM), lambda b, h, i: (b * nq + i, h)),
            pl.BlockSpec((seq, SB_HEAD_DIM), lambda b, h, i: (b, SB_HEADS + h)),
            pl.BlockSpec((seq, SB_HEAD_DIM), lambda b, h, i: (b, 2 * SB_HEADS + h)),
            pl.BlockSpec((rows, SB_HEAD_DIM), lambda b, h, i: (b * nq + i, 3 * SB_HEADS + h)),
        ],
        out_specs=pl.BlockSpec((rows, SB_HEAD_DIM), lambda b, h, i: (b * nq + i, h)),
        out_shape=jax.ShapeDtypeStruct((t, SB_HEADS * SB_HEAD_DIM), BF16),
        scratch_shapes=[
            pltpu.VMEM((ATTN_GROUP, SB_HEAD_DIM, ATTN_TQ), F32),
            pltpu.VMEM((ATTN_GROUP, 1, ATTN_TQ), F32),
        ],
        compiler_params=_params(("parallel", "parallel", "arbitrary")),
        name="stickbreak_attn",
    )(qkvg, qkvg, qkvg, qkvg)


EVEN_ALIGNED = 2 * D_INNER + BC_DIM
EVEN_AFTER_DT = EVEN_ALIGNED + SSD_HEADS


def _even_layer(h, norm_w, ev_w_in_t, ev_w_out, layer, conv_w, conv_b, dt_bias, a_log,
                d_skip, ssd_norm_w, sgu_ln_w, sgu_ln_b, sgu_w, sgu_b, *, batch, seq, in_place):
    tm, tn = IN_PROJ_TILE
    tiles = lambda cols: cols // tn
    proj, dt_raw = _in_proj(
        h, norm_w, ev_w_in_t, layer, tm=tm, tn=tn, transposed=True,
        segments=[(tiles(D_INNER), 0, 0),
                  (tiles(D_INNER + BC_DIM), D_INNER, tiles(4 * D_INNER)),
                  (tiles(3 * D_INNER), EVEN_AFTER_DT, tiles(D_INNER))],
        dt_start=EVEN_ALIGNED)
    y_a = _ssd_mixer(proj, dt_raw, conv_w, conv_b, dt_bias, a_log, d_skip, ssd_norm_w,
                     batch=batch, seq=seq)
    y_b = _sgu_mixer(proj, sgu_ln_w, sgu_ln_b, sgu_w, sgu_b)
    return _out_proj([y_a, y_b], [(ev_w_out, layer, 0), (ev_w_out, layer, 1)], h, None,
                     tm=EVEN_OUT_TILE[0], tn=EVEN_OUT_TILE[1], in_place=in_place)


def _odd_layer(h, norm_w, od_w_in, od_w_out, layer, final_w, *, batch, seq):
    scale = SB_HEAD_DIM ** -0.5 * math.log2(math.e)
    tm, tn = IN_PROJ_TILE
    qkvg = _in_proj(h, norm_w, od_w_in, layer, tm=tm, tn=tn,
                    segments=[(4 * D_INNER // tn, 0, 0)], scaled_tiles=D_INNER // tn, scale=scale)
    y = _attention(qkvg, batch=batch, seq=seq)
    tm, tn = ODD_OUT_TILE if final_w is None else FINAL_OUT_TILE
    return _out_proj([y], [(od_w_out, layer, 0)], h, final_w, tm=tm, tn=tn, in_place=True)


def kernel(x, norm_w, final_norm_w, ev_w_in, ev_conv_w, ev_conv_b, ev_dt_bias, ev_a_log,
           ev_d_skip, ev_ssd_norm_w, ev_sgu_ln_w, ev_sgu_ln_b, ev_sgu_w, ev_sgu_b, ev_w_out,
           od_w_in, od_w_out):
    batch, seq, d = x.shape
    depth = norm_w.shape[0]
    h = x.reshape(batch * seq, d)
    ev_w_in_t = jnp.swapaxes(ev_w_in, 1, 2)
    for layer in range(depth):
        i = layer // 2
        if layer % 2 == 0:
            h = _even_layer(h, norm_w[layer], ev_w_in_t, ev_w_out, i, ev_conv_w[i],
                            ev_conv_b[i], ev_dt_bias[i], ev_a_log[i], ev_d_skip[i],
                            ev_ssd_norm_w[i], ev_sgu_ln_w[i], ev_sgu_ln_b[i], ev_sgu_w[i],
                            ev_sgu_b[i], batch=batch, seq=seq, in_place=layer > 0)
        else:
            fw = final_norm_w if layer == depth - 1 else None
            h = _odd_layer(h, norm_w[layer], od_w_in, od_w_out, i, fw, batch=batch, seq=seq)
    if depth % 2 == 1:
        raise NotImplementedError("final RMSNorm is fused into an odd last layer")
    return h.reshape(batch, seq, d)
```

```python
import functools
import math

import jax
import jax.numpy as jnp
from jax import lax
from jax.experimental import pallas as pl
from jax.experimental.pallas import tpu as pltpu

F32 = jnp.float32
BF16 = jnp.bfloat16

D_MODEL = 2048
NORM_EPS = 1e-5
SSD_HEAD_DIM = 64
SSD_HEADS = 32
SSD_GROUPS = 4
SSD_STATE = 128
CHUNK = 128
SSD_CHUNKS_PER_STEP = 4
CONV_WIDTH = 4
D_INNER = 2048
BC_DIM = 2 * SSD_GROUPS * SSD_STATE
GROUP_COLS = D_INNER // SSD_GROUPS
SGU_GROUPS = 16
SGU_GROUP_DIM = 128
SB_HEADS = 16
SB_HEAD_DIM = 128
ATTN_TQ = 256
ATTN_TK = 128
ATTN_GROUP = 16
ATTN_SKIP_LOG2 = 160.0
DT_LANES = 128
DT_COPIES = 3

V7X_VMEM_LIMIT = 52 * 1024 * 1024
V7X_F32_SUBLANES = 8
CONV_HALO = V7X_F32_SUBLANES

IN_PROJ_TILE = (1024, 1024)
EVEN_OUT_TILE = (512, 1024)
ODD_OUT_TILE = (1024, 1024)
FINAL_OUT_TILE = (512, D_MODEL)


def _params(sem):
    return pltpu.CompilerParams(dimension_semantics=sem, vmem_limit_bytes=V7X_VMEM_LIMIT)


def _sigmoid(x):
    return 1.0 / (1.0 + jnp.exp(-x))


def _silu(x):
    return x * _sigmoid(x)


def _gelu_tanh(x):
    c, a = math.sqrt(2.0 / math.pi), 0.044715
    half = 0.5 * x
    return half + half * jnp.tanh(x * (c + (c * a) * (x * x)))


def _split3(x):
    h1 = x.astype(BF16)
    r1 = x - h1.astype(F32)
    h2 = r1.astype(BF16)
    r2 = r1 - h2.astype(F32)
    return h1, h2, r2.astype(BF16)


def _in_proj_kernel(*refs, transposed, scaled_tiles, scale, with_dt):
    x_ref, nw_ref, w_ref = refs[:3]
    hn_ref = refs[-1]
    if with_dt:
        wdt_ref, o_ref, dt_ref = refs[3:6]
    else:
        o_ref = refs[3]
    j = pl.program_id(1)

    def matmul(w_ref):
        w = w_ref[...].astype(BF16)
        if transposed:
            return lax.dot_general(hn_ref[...], w, (((1,), (1,)), ((), ())),
                                   preferred_element_type=F32)
        return jnp.dot(hn_ref[...], w, preferred_element_type=F32)

    @pl.when(j == 0)
    def _():
        x = x_ref[...]
        ms = jnp.mean(x * x, axis=-1, keepdims=True)
        hn_ref[...] = (x * lax.rsqrt(ms + NORM_EPS) * nw_ref[...]).astype(BF16)
        if with_dt:
            dt_ref[...] = matmul(wdt_ref)

    res = matmul(w_ref)
    if scaled_tiles:
        res = res * jnp.where(j < scaled_tiles, scale, 1.0)
    o_ref[...] = res.astype(o_ref.dtype)


def _in_proj(h, norm_w, w, layer, *, tm, tn, segments, transposed=False, scaled_tiles=0,
             scale=1.0, dt_start=None):
    m, d = h.shape
    n_total = sum(cnt for cnt, _, _ in segments)

    def pick(j, field, unit=1):
        out, first = 0, 0
        for seg in segments:
            inside = jnp.logical_and(j >= first, j < first + seg[0])
            step = tn // unit if field == 1 else 1
            out = out + jnp.where(inside, (j - first) * step + seg[field] // (unit if field == 1 else 1), 0)
            first += seg[0]
        return out

    if transposed:
        n_cols = w.shape[1]
        w = w.reshape(-1, d)
        sub = V7X_F32_SUBLANES
        assert n_cols % sub == 0 and all(first % sub == 0 for _, first, _ in segments)
        w_spec = pl.BlockSpec((pl.Element(tn), pl.Element(d)),
                              lambda i, j: (sub * (layer * n_cols // sub + pick(j, 1, sub)), 0))
    else:
        assert all(first % tn == 0 for _, first, _ in segments)
        w_spec = pl.BlockSpec((None, d, tn), lambda i, j: (layer, 0, pick(j, 1, tn)))
    in_specs = [pl.BlockSpec((tm, d), lambda i, j: (i, 0)),
                pl.BlockSpec((1, d), lambda i, j: (0, 0)),
                w_spec]
    args = [h, norm_w.reshape(1, d), w]
    out_specs = pl.BlockSpec((tm, tn), lambda i, j: (i, pick(j, 2)))
    out_shape = jax.ShapeDtypeStruct((m, n_total * tn), BF16)
    if dt_start is not None:
        assert transposed
        in_specs.append(pl.BlockSpec((pl.Element(DT_LANES), pl.Element(d)),
                                     lambda i, j: (sub * ((layer * n_cols + dt_start) // sub), 0)))
        args.append(w)
        out_specs = [out_specs, pl.BlockSpec((tm, DT_LANES), lambda i, j: (i, 0))]
        out_shape = [out_shape, jax.ShapeDtypeStruct((m, DT_LANES), F32)]
    return pl.pallas_call(
        functools.partial(_in_proj_kernel, transposed=transposed, scaled_tiles=scaled_tiles,
                          scale=scale, with_dt=dt_start is not None),
        grid=(m // tm, n_total),
        in_specs=in_specs,
        out_specs=out_specs,
        out_shape=out_shape,
        scratch_shapes=[pltpu.VMEM((tm, d), BF16)],
        compiler_params=_params(("parallel", "arbitrary")),
        name="norm_in_proj",
    )(*args)


def _out_proj_kernel(*refs, n_in, final_norm):
    y_refs = refs[:n_in]
    w_refs = refs[n_in:2 * n_in]
    h_ref = refs[2 * n_in]
    o_ref, wb_ref = refs[-2:]

    @pl.when(pl.program_id(1) == 0)
    def _():
        for k, w_ref in enumerate(w_refs):
            wb_ref[k] = w_ref[...].astype(BF16)

    acc = h_ref[...]
    for k, y_ref in enumerate(y_refs):
        acc = acc + jnp.dot(y_ref[...], wb_ref[k], preferred_element_type=F32)
    if final_norm:
        fw_ref = refs[2 * n_in + 1]
        ms = jnp.mean(acc * acc, axis=-1, keepdims=True)
        acc = acc * lax.rsqrt(ms + NORM_EPS) * fw_ref[...]
    o_ref[...] = acc


def _out_proj(ys, ws, h, final_w, *, tm, tn, in_place):
    m, n = h.shape
    n_in = len(ys)
    kdim = ys[0].shape[1]
    assert all(y.shape[1] == kdim for y in ys)
    final_norm = final_w is not None
    if final_norm:
        assert tn == n

    def w_spec(layer, row_block):
        return pl.BlockSpec((None, kdim, tn), lambda j, i: (layer, row_block, j),
                            pipeline_mode=pl.Buffered(1))

    in_specs = [pl.BlockSpec((tm, kdim), lambda j, i: (i, 0)) for _ in ys]
    in_specs += [w_spec(layer, rb) for _, layer, rb in ws]
    in_specs += [pl.BlockSpec((tm, tn), lambda j, i: (i, j))]
    args = list(ys) + [w for w, _, _ in ws] + [h]
    if final_norm:
        in_specs += [pl.BlockSpec((1, n), lambda j, i: (0, 0))]
        args += [final_w.reshape(1, n)]
    return pl.pallas_call(
        functools.partial(_out_proj_kernel, n_in=n_in, final_norm=final_norm),
        grid=(n // tn, m // tm),
        in_specs=in_specs,
        out_specs=pl.BlockSpec((tm, tn), lambda j, i: (i, j)),
        out_shape=jax.ShapeDtypeStruct((m, n), F32),
        scratch_shapes=[pltpu.VMEM((n_in, kdim, tn), BF16)],
        input_output_aliases={2 * n_in: 0} if in_place else {},
        compiler_params=_params(("arbitrary", "arbitrary")),
        name="out_proj",
    )(*args)


def _causal_conv(ext_ref, tail_ref, cur, w_ref, b_ref):
    ext_ref[0:CONV_HALO, :] = tail_ref[...]
    ext_ref[CONV_HALO:CONV_HALO + CHUNK, :] = cur
    tail_ref[...] = cur[CHUNK - CONV_HALO:, :]
    acc = b_ref[...] + w_ref[CONV_WIDTH - 1:CONV_WIDTH, :] * cur
    for k in range(CONV_WIDTH - 1):
        d = CONV_WIDTH - 1 - k
        acc = acc + w_ref[k:k + 1, :] * ext_ref[CONV_HALO - d:CONV_HALO - d + CHUNK, :]
    return acc


def _ssd_kernel(z_ref, x_ref, bc_ref, dt_ref, cwx_ref, cbx_ref, cwbc_ref, cbbc_ref,
                dtb_ref, alog_ref, dskip_ref, nw_ref, e64_ref, e128_ref,
                o_ref,
                state_ref, xtail_ref, bctail_ref, xext_ref, bcext_ref, y_ref):
    c = pl.program_id(1)

    @pl.when(c == 0)
    def _():
        state_ref[...] = jnp.zeros_like(state_ref)
        xtail_ref[...] = jnp.zeros_like(xtail_ref)
        bctail_ref[...] = jnp.zeros_like(bctail_ref)

    L = CHUNK
    row = lax.broadcasted_iota(jnp.int32, (L, L), 0)
    col = lax.broadcasted_iota(jnp.int32, (L, L), 1)
    causal = row >= col
    tri = jnp.where(causal, 1.0, 0.0).astype(BF16)
    head_lane = lax.broadcasted_iota(jnp.int32, (L, DT_LANES), 1) < SSD_HEADS

    def one_chunk(rows):
        dt_in = dt_ref[rows, :] + dtb_ref[...]
        dt = jnp.where(head_lane, jnp.maximum(dt_in, 0.0) + jnp.log1p(jnp.exp(-jnp.abs(dt_in))), 0.0)
        a_dt = dt * (-jnp.exp(alog_ref[...]))
        p1, p2, p3 = _split3(a_dt)
        cs = (jnp.dot(tri, p1, preferred_element_type=F32)
              + jnp.dot(tri, p2, preferred_element_type=F32)
              + jnp.dot(tri, p3, preferred_element_type=F32))
        cs_t = cs.T

        def lane_pieces(v):
            q1, q2, q3 = (q.astype(F32) for q in _split3(v))
            return (q1 + pltpu.roll(q2, SSD_HEADS, 1) + pltpu.roll(q3, 2 * SSD_HEADS, 1)).astype(BF16)

        dt_p = lane_pieces(dt)
        cs_p = lane_pieces(cs)
        e64 = e64_ref[...]
        dt_e = jnp.dot(dt_p, e64, preferred_element_type=F32)
        cs_e = jnp.dot(cs_p, e64, preferred_element_type=F32)
        cs_col = jnp.dot(cs_p, e128_ref[...], preferred_element_type=F32)

        xs = _silu(_causal_conv(xext_ref, xtail_ref, x_ref[rows, :].astype(F32), cwx_ref, cbx_ref))
        bc = _silu(_causal_conv(bcext_ref, bctail_ref, bc_ref[rows, :].astype(F32), cwbc_ref, cbbc_ref))

        cs_last = cs_e[L - 1:L, :]
        x_dt = xs * dt_e
        x_dt_b = x_dt.astype(BF16)
        x_end_b = (x_dt * jnp.exp(cs_last - cs_e)).astype(BF16)
        in_decay = jnp.exp(cs_e)
        chunk_decay = jnp.exp(cs_last)

        lane_l = lax.broadcasted_iota(jnp.int32, (L, 2 * SSD_HEAD_DIM), 1)
        first_head = lane_l < SSD_HEAD_DIM
        ssq = jnp.zeros((L, 1), F32)
        for g in range(SSD_GROUPS):
            gs = slice(g * GROUP_COLS, (g + 1) * GROUP_COLS)
            b_g = bc[:, g * SSD_STATE:(g + 1) * SSD_STATE].astype(BF16)
            c_g = bc[:, (SSD_GROUPS + g) * SSD_STATE:(SSD_GROUPS + g + 1) * SSD_STATE].astype(BF16)
            scores = lax.dot_general(c_g, b_g, (((1,), (1,)), ((), ())),
                                     preferred_element_type=F32)
            heads_per_group = SSD_HEADS // SSD_GROUPS
            y_parts = []
            for pair in range(heads_per_group // 2):
                ms = []
                for r in range(2):
                    h = g * heads_per_group + 2 * pair + r
                    seg = cs_col[:, h * L:(h + 1) * L] - cs_t[h:h + 1, :]
                    decay = jnp.exp(jnp.where(causal, seg, -jnp.inf))
                    ms.append((scores * decay).astype(BF16))
                lhs = jnp.concatenate(ms, axis=1)
                c0 = g * GROUP_COLS + pair * 2 * SSD_HEAD_DIM
                xp = x_dt_b[:, c0:c0 + 2 * SSD_HEAD_DIM]
                zero = jnp.zeros_like(xp)
                rhs = jnp.concatenate([jnp.where(first_head, xp, zero),
                                       jnp.where(first_head, zero, xp)], axis=0)
                y_parts.append(jnp.dot(lhs, rhs, preferred_element_type=F32))
            y_diag = jnp.concatenate(y_parts, axis=1)

            st = state_ref[:, gs]
            y_off = jnp.dot(c_g, st.astype(BF16), preferred_element_type=F32) * in_decay[:, gs]
            new_st = lax.dot_general(b_g, x_end_b[:, gs], (((0,), (0,)), ((), ())),
                                     preferred_element_type=F32)
            state_ref[:, gs] = chunk_decay[:, gs] * st + new_st

            y = y_diag + y_off + xs[:, gs] * dskip_ref[:, gs]
            y = y * _silu(z_ref[rows, gs].astype(F32))
            ssq = ssq + jnp.sum(y * y, axis=-1, keepdims=True)
            y_ref[:, gs] = y

        inv = lax.rsqrt(ssq * (1.0 / D_INNER) + NORM_EPS)
        o_ref[rows, :] = (y_ref[...] * inv * nw_ref[...]).astype(o_ref.dtype)

    for cc in range(SSD_CHUNKS_PER_STEP):
        one_chunk(slice(cc * CHUNK, (cc + 1) * CHUNK))


def _ssd_mixer(proj, dt_raw, conv_w, conv_b, dt_bias, a_log, d_skip, norm_w, *, batch, seq):
    rows = SSD_CHUNKS_PER_STEP * CHUNK
    nc = seq // rows
    t = batch * seq
    lanes = jnp.arange(DT_LANES)
    valid = lanes < DT_COPIES * SSD_HEADS
    head_of_lane = lanes % SSD_HEADS

    def tile_heads(v):
        return jnp.pad(v.astype(F32), (0, DT_LANES - SSD_HEADS)).reshape(1, DT_LANES)

    def expand(width):
        cols = jnp.arange(SSD_HEADS * width) // width
        return ((head_of_lane[:, None] == cols[None, :]) & valid[:, None]).astype(BF16)

    e64 = expand(SSD_HEAD_DIM)
    e128 = expand(CHUNK)
    d_skip_e = jnp.repeat(d_skip, SSD_HEAD_DIM).reshape(1, D_INNER)

    tok = lambda b, c: b * nc + c
    const = lambda b, c: (0, 0)
    full = lambda a: pl.BlockSpec(a.shape, const)
    cwx, cwbc = conv_w[:, :D_INNER], conv_w[:, D_INNER:]
    cbx, cbbc = conv_b[:D_INNER].reshape(1, -1), conv_b[D_INNER:].reshape(1, -1)
    dtb, alog = tile_heads(dt_bias), tile_heads(a_log)
    nw = norm_w.reshape(1, D_INNER)
    params = [cwx, cbx, cwbc, cbbc, dtb, alog, d_skip_e, nw, e64, e128]
    return pl.pallas_call(
        _ssd_kernel,
        grid=(batch, nc),
        in_specs=[
            pl.BlockSpec((rows, D_INNER), lambda b, c: (tok(b, c), 0)),
            pl.BlockSpec((rows, D_INNER), lambda b, c: (tok(b, c), 4)),
            pl.BlockSpec((rows, BC_DIM), lambda b, c: (tok(b, c), 10)),
            pl.BlockSpec((rows, DT_LANES), lambda b, c: (tok(b, c), 0)),
        ] + [full(p) for p in params],
        out_specs=pl.BlockSpec((rows, D_INNER), lambda b, c: (tok(b, c), 0)),
        out_shape=jax.ShapeDtypeStruct((t, D_INNER), BF16),
        scratch_shapes=[
            pltpu.VMEM((SSD_STATE, D_INNER), F32),
            pltpu.VMEM((CONV_HALO, D_INNER), F32),
            pltpu.VMEM((CONV_HALO, BC_DIM), F32),
            pltpu.VMEM((CONV_HALO + CHUNK, D_INNER), F32),
            pltpu.VMEM((CONV_HALO + CHUNK, BC_DIM), F32),
            pltpu.VMEM((CHUNK, D_INNER), F32),
        ],
        compiler_params=_params(("parallel", "arbitrary")),
        name="ssd_mixer",
    )(proj, proj, proj, dt_raw, *params)


def _sgu_kernel(g_ref, u_ref, v_ref, lnw_ref, lnb_ref, ws_ref, bias_ref, o_ref, wtri_ref):
    @pl.when(pl.program_id(0) == 0)
    def _():
        row = lax.broadcasted_iota(jnp.int32, (CHUNK, CHUNK), 0)
        col = lax.broadcasted_iota(jnp.int32, (CHUNK, CHUNK), 1)
        for g in range(SGU_GROUPS):
            wtri_ref[g] = jnp.where(row >= col, ws_ref[g], 0.0).astype(BF16)

    v = _gelu_tanh(v_ref[...].astype(F32))
    mu = jnp.mean(v, axis=-1, keepdims=True)
    vc = v - mu
    var = jnp.mean(vc * vc, axis=-1, keepdims=True)
    vn = (vc * lax.rsqrt(var + NORM_EPS) * lnw_ref[...] + lnb_ref[...]).astype(BF16)
    for g in range(SGU_GROUPS):
        gs = slice(g * SGU_GROUP_DIM, (g + 1) * SGU_GROUP_DIM)
        mixed = jnp.dot(wtri_ref[g], vn[:, gs], preferred_element_type=F32) + bias_ref[:, gs]
        u = _gelu_tanh(u_ref[:, gs].astype(F32))
        o_ref[:, gs] = (u * mixed * _silu(g_ref[:, gs].astype(F32))).astype(o_ref.dtype)


def _sgu_mixer(proj, ln_w, ln_b, w_s, b_s):
    t = proj.shape[0]
    bias = jnp.repeat(b_s.T, SGU_GROUP_DIM, axis=1)
    const2 = lambda i: (0, 0)
    return pl.pallas_call(
        _sgu_kernel,
        grid=(t // CHUNK,),
        in_specs=[
            pl.BlockSpec((CHUNK, D_INNER), lambda i: (i, 1)),
            pl.BlockSpec((CHUNK, D_INNER), lambda i: (i, 2)),
            pl.BlockSpec((CHUNK, D_INNER), lambda i: (i, 3)),
            pl.BlockSpec((1, D_INNER), const2),
            pl.BlockSpec((1, D_INNER), const2),
            pl.BlockSpec(w_s.shape, lambda i: (0, 0, 0)),
            pl.BlockSpec((CHUNK, D_INNER), const2),
        ],
        out_specs=pl.BlockSpec((CHUNK, D_INNER), lambda i: (i, 0)),
        out_shape=jax.ShapeDtypeStruct((t, D_INNER), BF16),
        scratch_shapes=[pltpu.VMEM((SGU_GROUPS, CHUNK, CHUNK), BF16)],
        compiler_params=_params(("arbitrary",)),
        name="sgu_mixer",
    )(proj, proj, proj, ln_w.reshape(1, -1), ln_b.reshape(1, -1), w_s, bias)


def _attn_kernel(q_ref, k_ref, v_ref, g_ref, o_ref, acc_ref, carry_ref):
    i = pl.program_id(2)
    tq, tk = ATTN_TQ, ATTN_TK
    srow = lax.broadcasted_iota(jnp.int32, (tk, tk), 0)
    scol = lax.broadcasted_iota(jnp.int32, (tk, tk), 1)
    later = jnp.where(scol > srow, 1.0, 0.0).astype(BF16)

    def softplus2(z):
        return jnp.maximum(z, 0.0) + jnp.log2(1.0 + jnp.exp2(-jnp.abs(z)))

    def later_sums(sp, nsub):
        hi = sp.astype(BF16)
        afts, tots = [], []
        for r in range(nsub):
            aft = jnp.dot(later, hi[r * tk:(r + 1) * tk], preferred_element_type=F32)
            afts.append(aft)
            tots.append(aft[0:1, :] + sp[r * tk:r * tk + 1, :])
        return afts, tots

    def block_weights(z, sp, afts, tots, carry, nsub):
        ws = [None] * nsub
        for r in reversed(range(nsub)):
            rs = slice(r * tk, (r + 1) * tk)
            ws[r] = jnp.exp2((z[rs] - sp[rs]) - afts[r] - carry).astype(BF16)
            carry = carry + tots[r]
        return jnp.concatenate(ws, axis=0), carry

    def values(ks, nkeys, w):
        return lax.dot_general(v_ref[pl.ds(ks, nkeys), :], w, (((0,), (0,)), ((), ())),
                               preferred_element_type=F32)

    tiles = range(ATTN_GROUP)
    nsub = tq // tk
    tile_idx = [i * ATTN_GROUP + g for g in tiles]
    diag_start = [pl.multiple_of(t * tq, tq) for t in tile_idx]
    prev_start = [pl.multiple_of(jnp.maximum(t - 1, 0) * tq, tq) for t in tile_idx]
    keep = [jnp.where(t > 0, 1.0, 0.0) for t in tile_idx]
    below = (lax.broadcasted_iota(jnp.int32, (tq, tq), 0)
             < lax.broadcasted_iota(jnp.int32, (tq, tq), 1))

    def scores(g, ks):
        return lax.dot_general(k_ref[pl.ds(ks, tq), :], q_ref[g * tq:(g + 1) * tq, :],
                               (((1,), (1,)), ((), ())), preferred_element_type=F32)

    z_diag = [scores(g, diag_start[g]) for g in tiles]
    z_prev = [scores(g, prev_start[g]) for g in tiles]
    sp_diag = [jnp.where(below, softplus2(z), 0.0) for z in z_diag]
    sp_prev = [softplus2(z_prev[g]) * keep[g] for g in tiles]
    sums_diag = [later_sums(sp, nsub) for sp in sp_diag]
    sums_prev = [later_sums(sp, nsub) for sp in sp_prev]
    for g in tiles:
        w, carry = block_weights(z_diag[g], sp_diag[g], *sums_diag[g], jnp.zeros((1, tq), F32), nsub)
        w = jnp.where(below, w, jnp.zeros_like(w))
        acc = values(diag_start[g], tq, w)
        w, carry = block_weights(z_prev[g], sp_prev[g], *sums_prev[g], carry, nsub)
        w = jnp.where(tile_idx[g] > 0, w, jnp.zeros_like(w))
        acc_ref[g] = acc + values(prev_start[g], tq, w)
        carry_ref[g] = carry

    for g in tiles:
        def more(state):
            j, least = state
            return jnp.logical_and(j >= 0, least < ATTN_SKIP_LOG2)

        def earlier(state, g=g):
            j, _ = state
            ks = pl.multiple_of(j * tq, tq)
            z = lax.dot_general(k_ref[pl.ds(ks, tq), :], q_ref[g * tq:(g + 1) * tq, :],
                                (((1,), (1,)), ((), ())), preferred_element_type=F32)
            sp = softplus2(z)
            afts, tots = later_sums(sp, tq // tk)
            w, carry = block_weights(z, sp, afts, tots, carry_ref[g], tq // tk)
            acc_ref[g] += values(ks, tq, w)
            carry_ref[g] = carry
            return j - 1, jnp.min(carry)

        lax.while_loop(more, earlier, (tile_idx[g] - 2, jnp.min(carry_ref[g])))

    for g in tiles:
        rows = slice(g * tq, (g + 1) * tq)
        o_ref[rows, :] = (acc_ref[g].T * _silu(g_ref[rows, :].astype(F32))).astype(o_ref.dtype)


def _attention(qkvg, *, batch, seq):
    rows = ATTN_GROUP * ATTN_TQ
    assert seq % rows == 0
    nq = seq // rows
    t = batch * seq
    return pl.pallas_call(
        _attn_kernel,
        grid=(batch, SB_HEADS, nq),
        in_specs=[
            pl.BlockSpec((rows, SB_HEAD_DIM), lambda b, h, i: (b * nq + i, h)),
            pl.BlockSpec((seq, SB_HEAD_DIM), lambda b, h, i: (b, SB_HEADS + h)),
            pl.BlockSpec((seq, SB_HEAD_DIM), lambda b, h, i: (b, 2 * SB_HEADS + h)),
            pl.BlockSpec((rows, SB_HEAD_DIM), lambda b, h, i: (b * nq + i, 3 * SB_HEADS + h)),
        ],
        out_specs=pl.BlockSpec((rows, SB_HEAD_DIM), lambda b, h, i: (b * nq + i, h)),
        out_shape=jax.ShapeDtypeStruct((t, SB_HEADS * SB_HEAD_DIM), BF16),
        scratch_shapes=[
            pltpu.VMEM((ATTN_GROUP, SB_HEAD_DIM, ATTN_TQ), F32),
            pltpu.VMEM((ATTN_GROUP, 1, ATTN_TQ), F32),
        ],
        compiler_params=_params(("parallel", "parallel", "arbitrary")),
        name="stickbreak_attn",
    )(qkvg, qkvg, qkvg, qkvg)


EVEN_ALIGNED = 2 * D_INNER + BC_DIM
EVEN_AFTER_DT = EVEN_ALIGNED + SSD_HEADS


def _even_layer(h, norm_w, ev_w_in_t, ev_w_out, layer, conv_w, conv_b, dt_bias, a_log,
                d_skip, ssd_norm_w, sgu_ln_w, sgu_ln_b, sgu_w, sgu_b, *, batch, seq, in_place):
    tm, tn = IN_PROJ_TILE
    tiles = lambda cols: cols // tn
    proj, dt_raw = _in_proj(
        h, norm_w, ev_w_in_t, layer, tm=tm, tn=tn, transposed=True,
        segments=[(tiles(D_INNER), 0, 0),
                  (tiles(D_INNER + BC_DIM), D_INNER, tiles(4 * D_INNER)),
                  (tiles(3 * D_INNER), EVEN_AFTER_DT, tiles(D_INNER))],
        dt_start=EVEN_ALIGNED)
    y_a = _ssd_mixer(proj, dt_raw, conv_w, conv_b, dt_bias, a_log, d_skip, ssd_norm_w,
                     batch=batch, seq=seq)
    y_b = _sgu_mixer(proj, sgu_ln_w, sgu_ln_b, sgu_w, sgu_b)
    return _out_proj([y_a, y_b], [(ev_w_out, layer, 0), (ev_w_out, layer, 1)], h, None,
                     tm=EVEN_OUT_TILE[0], tn=EVEN_OUT_TILE[1], in_place=in_place)


def _odd_layer(h, norm_w, od_w_in, od_w_out, layer, final_w, *, batch, seq):
    scale = SB_HEAD_DIM ** -0.5 * math.log2(math.e)
    tm, tn = IN_PROJ_TILE
    qkvg = _in_proj(h, norm_w, od_w_in, layer, tm=tm, tn=tn,
                    segments=[(4 * D_INNER // tn, 0, 0)], scaled_tiles=D_INNER // tn, scale=scale)
    y = _attention(qkvg, batch=batch, seq=seq)
    tm, tn = ODD_OUT_TILE if final_w is None else FINAL_OUT_TILE
    return _out_proj([y], [(od_w_out, layer, 0)], h, final_w, tm=tm, tn=tn, in_place=True)


def kernel(x, norm_w, final_norm_w, ev_w_in, ev_conv_w, ev_conv_b, ev_dt_bias, ev_a_log,
           ev_d_skip, ev_ssd_norm_w, ev_sgu_ln_w, ev_sgu_ln_b, ev_sgu_w, ev_sgu_b, ev_w_out,
           od_w_in, od_w_out):
    batch, seq, d = x.shape
    depth = norm_w.shape[0]
    h = x.reshape(batch * seq, d)
    ev_w_in_t = jnp.swapaxes(ev_w_in, 1, 2)
    for layer in range(depth):
        i = layer // 2
        if layer % 2 == 0:
            h = _even_layer(h, norm_w[layer], ev_w_in_t, ev_w_out, i, ev_conv_w[i],
                            ev_conv_b[i], ev_dt_bias[i], ev_a_log[i], ev_d_skip[i],
                            ev_ssd_norm_w[i], ev_sgu_ln_w[i], ev_sgu_ln_b[i], ev_sgu_w[i],
                            ev_sgu_b[i], batch=batch, seq=seq, in_place=layer > 0)
        else:
            fw = final_norm_w if layer == depth - 1 else None
            h = _odd_layer(h, norm_w[layer], od_w_in, od_w_out, i, fw, batch=batch, seq=seq)
    if depth % 2 == 1:
        raise NotImplementedError("final RMSNorm is fused into an odd last layer")
    return h.reshape(batch, seq, d)
```

```python
import functools
import math

import jax
import jax.numpy as jnp
from jax import lax
from jax.experimental import pallas as pl
from jax.experimental.pallas import tpu as pltpu

F32 = jnp.float32
BF16 = jnp.bfloat16

D_MODEL = 2048
NORM_EPS = 1e-5
SSD_HEAD_DIM = 64
SSD_HEADS = 32
SSD_GROUPS = 4
SSD_STATE = 128
CHUNK = 128
SSD_CHUNKS_PER_STEP = 4
CONV_WIDTH = 4
D_INNER = 2048
BC_DIM = 2 * SSD_GROUPS * SSD_STATE
GROUP_COLS = D_INNER // SSD_GROUPS
SGU_GROUPS = 16
SGU_GROUP_DIM = 128
SB_HEADS = 16
SB_HEAD_DIM = 128
ATTN_TQ = 512
ATTN_TK = 128
ATTN_SUB = ATTN_TQ // ATTN_TK
DT_LANES = 128
DT_COPIES = 3

V7X_VMEM_LIMIT = 52 * 1024 * 1024
V7X_F32_SUBLANES = 8
CONV_HALO = V7X_F32_SUBLANES

IN_PROJ_TILE = (1024, 1024)
EVEN_OUT_TILE = (512, 1024)
ODD_OUT_TILE = (1024, 1024)
FINAL_OUT_TILE = (512, D_MODEL)


def _params(sem):
    return pltpu.CompilerParams(dimension_semantics=sem, vmem_limit_bytes=V7X_VMEM_LIMIT)


def _sigmoid(x):
    return 1.0 / (1.0 + jnp.exp(-x))


def _silu(x):
    return x * _sigmoid(x)


def _gelu_tanh(x):
    c, a = math.sqrt(2.0 / math.pi), 0.044715
    half = 0.5 * x
    return half + half * jnp.tanh(x * (c + (c * a) * (x * x)))


def _split3(x):
    h1 = x.astype(BF16)
    r1 = x - h1.astype(F32)
    h2 = r1.astype(BF16)
    r2 = r1 - h2.astype(F32)
    return h1, h2, r2.astype(BF16)


def _in_proj_kernel(*refs, transposed, scaled_tiles, scale, with_dt):
    x_ref, nw_ref, w_ref = refs[:3]
    hn_ref = refs[-1]
    if with_dt:
        wdt_ref, o_ref, dt_ref = refs[3:6]
    else:
        o_ref = refs[3]
    j = pl.program_id(1)

    def matmul(w_ref):
        w = w_ref[...].astype(BF16)
        if transposed:
            return lax.dot_general(hn_ref[...], w, (((1,), (1,)), ((), ())),
                                   preferred_element_type=F32)
        return jnp.dot(hn_ref[...], w, preferred_element_type=F32)

    @pl.when(j == 0)
    def _():
        x = x_ref[...]
        ms = jnp.mean(x * x, axis=-1, keepdims=True)
        hn_ref[...] = (x * lax.rsqrt(ms + NORM_EPS) * nw_ref[...]).astype(BF16)
        if with_dt:
            dt_ref[...] = matmul(wdt_ref)

    res = matmul(w_ref)
    if scaled_tiles:
        res = res * jnp.where(j < scaled_tiles, scale, 1.0)
    o_ref[...] = res.astype(o_ref.dtype)


def _in_proj(h, norm_w, w, layer, *, tm, tn, segments, transposed=False, scaled_tiles=0,
             scale=1.0, dt_start=None):
    m, d = h.shape
    n_total = sum(cnt for cnt, _, _ in segments)

    def pick(j, field, unit=1):
        out, first = 0, 0
        for seg in segments:
            inside = jnp.logical_and(j >= first, j < first + seg[0])
            step = tn // unit if field == 1 else 1
            out = out + jnp.where(inside, (j - first) * step + seg[field] // (unit if field == 1 else 1), 0)
            first += seg[0]
        return out

    if transposed:
        n_cols = w.shape[1]
        w = w.reshape(-1, d)
        sub = V7X_F32_SUBLANES
        assert n_cols % sub == 0 and all(first % sub == 0 for _, first, _ in segments)
        w_spec = pl.BlockSpec((pl.Element(tn), pl.Element(d)),
                              lambda i, j: (sub * (layer * n_cols // sub + pick(j, 1, sub)), 0))
    else:
        assert all(first % tn == 0 for _, first, _ in segments)
        w_spec = pl.BlockSpec((None, d, tn), lambda i, j: (layer, 0, pick(j, 1, tn)))
    in_specs = [pl.BlockSpec((tm, d), lambda i, j: (i, 0)),
                pl.BlockSpec((1, d), lambda i, j: (0, 0)),
                w_spec]
    args = [h, norm_w.reshape(1, d), w]
    out_specs = pl.BlockSpec((tm, tn), lambda i, j: (i, pick(j, 2)))
    out_shape = jax.ShapeDtypeStruct((m, n_total * tn), BF16)
    if dt_start is not None:
        assert transposed
        in_specs.append(pl.BlockSpec((pl.Element(DT_LANES), pl.Element(d)),
                                     lambda i, j: (sub * ((layer * n_cols + dt_start) // sub), 0)))
        args.append(w)
        out_specs = [out_specs, pl.BlockSpec((tm, DT_LANES), lambda i, j: (i, 0))]
        out_shape = [out_shape, jax.ShapeDtypeStruct((m, DT_LANES), F32)]
    return pl.pallas_call(
        functools.partial(_in_proj_kernel, transposed=transposed, scaled_tiles=scaled_tiles,
                          scale=scale, with_dt=dt_start is not None),
        grid=(m // tm, n_total),
        in_specs=in_specs,
        out_specs=out_specs,
        out_shape=out_shape,
        scratch_shapes=[pltpu.VMEM((tm, d), BF16)],
        compiler_params=_params(("parallel", "arbitrary")),
        name="norm_in_proj",
    )(*args)


def _out_proj_kernel(*refs, n_in, final_norm):
    y_refs = refs[:n_in]
    w_refs = refs[n_in:2 * n_in]
    h_ref = refs[2 * n_in]
    o_ref, wb_ref = refs[-2:]

    @pl.when(pl.program_id(1) == 0)
    def _():
        for k, w_ref in enumerate(w_refs):
            wb_ref[k] = w_ref[...].astype(BF16)

    acc = h_ref[...]
    for k, y_ref in enumerate(y_refs):
        acc = acc + jnp.dot(y_ref[...], wb_ref[k], preferred_element_type=F32)
    if final_norm:
        fw_ref = refs[2 * n_in + 1]
        ms = jnp.mean(acc * acc, axis=-1, keepdims=True)
        acc = acc * lax.rsqrt(ms + NORM_EPS) * fw_ref[...]
    o_ref[...] = acc


def _out_proj(ys, ws, h, final_w, *, tm, tn, in_place):
    m, n = h.shape
    n_in = len(ys)
    kdim = ys[0].shape[1]
    assert all(y.shape[1] == kdim for y in ys)
    final_norm = final_w is not None
    if final_norm:
        assert tn == n

    def w_spec(layer, row_block):
        return pl.BlockSpec((None, kdim, tn), lambda j, i: (layer, row_block, j),
                            pipeline_mode=pl.Buffered(1))

    in_specs = [pl.BlockSpec((tm, kdim), lambda j, i: (i, 0)) for _ in ys]
    in_specs += [w_spec(layer, rb) for _, layer, rb in ws]
    in_specs += [pl.BlockSpec((tm, tn), lambda j, i: (i, j))]
    args = list(ys) + [w for w, _, _ in ws] + [h]
    if final_norm:
        in_specs += [pl.BlockSpec((1, n), lambda j, i: (0, 0))]
        args += [final_w.reshape(1, n)]
    return pl.pallas_call(
        functools.partial(_out_proj_kernel, n_in=n_in, final_norm=final_norm),
        grid=(n // tn, m // tm),
        in_specs=in_specs,
        out_specs=pl.BlockSpec((tm, tn), lambda j, i: (i, j)),
        out_shape=jax.ShapeDtypeStruct((m, n), F32),
        scratch_shapes=[pltpu.VMEM((n_in, kdim, tn), BF16)],
        input_output_aliases={2 * n_in: 0} if in_place else {},
        compiler_params=_params(("arbitrary", "arbitrary")),
        name="out_proj",
    )(*args)


def _causal_conv(ext_ref, tail_ref, cur, w_ref, b_ref):
    ext_ref[0:CONV_HALO, :] = tail_ref[...]
    ext_ref[CONV_HALO:CONV_HALO + CHUNK, :] = cur
    tail_ref[...] = cur[CHUNK - CONV_HALO:, :]
    acc = b_ref[...] + w_ref[CONV_WIDTH - 1:CONV_WIDTH, :] * cur
    for k in range(CONV_WIDTH - 1):
        d = CONV_WIDTH - 1 - k
        acc = acc + w_ref[k:k + 1, :] * ext_ref[CONV_HALO - d:CONV_HALO - d + CHUNK, :]
    return acc


def _ssd_kernel(z_ref, x_ref, bc_ref, dt_ref, cwx_ref, cbx_ref, cwbc_ref, cbbc_ref,
                dtb_ref, alog_ref, dskip_ref, nw_ref, e64_ref, e128_ref,
                o_ref,
                state_ref, xtail_ref, bctail_ref, xext_ref, bcext_ref, y_ref):
    c = pl.program_id(1)

    @pl.when(c == 0)
    def _():
        state_ref[...] = jnp.zeros_like(state_ref)
        xtail_ref[...] = jnp.zeros_like(xtail_ref)
        bctail_ref[...] = jnp.zeros_like(bctail_ref)

    L = CHUNK
    row = lax.broadcasted_iota(jnp.int32, (L, L), 0)
    col = lax.broadcasted_iota(jnp.int32, (L, L), 1)
    causal = row >= col
    tri = jnp.where(causal, 1.0, 0.0).astype(BF16)
    head_lane = lax.broadcasted_iota(jnp.int32, (L, DT_LANES), 1) < SSD_HEADS

    def one_chunk(rows):
        dt_in = dt_ref[rows, :] + dtb_ref[...]
        dt = jnp.where(head_lane, jnp.maximum(dt_in, 0.0) + jnp.log1p(jnp.exp(-jnp.abs(dt_in))), 0.0)
        a_dt = dt * (-jnp.exp(alog_ref[...]))
        p1, p2, p3 = _split3(a_dt)
        cs = (jnp.dot(tri, p1, preferred_element_type=F32)
              + jnp.dot(tri, p2, preferred_element_type=F32)
              + jnp.dot(tri, p3, preferred_element_type=F32))
        cs_t = cs.T

        def lane_pieces(v):
            q1, q2, q3 = (q.astype(F32) for q in _split3(v))
            return (q1 + pltpu.roll(q2, SSD_HEADS, 1) + pltpu.roll(q3, 2 * SSD_HEADS, 1)).astype(BF16)

        dt_p = lane_pieces(dt)
        cs_p = lane_pieces(cs)
        e64 = e64_ref[...]
        dt_e = jnp.dot(dt_p, e64, preferred_element_type=F32)
        cs_e = jnp.dot(cs_p, e64, preferred_element_type=F32)
        cs_col = jnp.dot(cs_p, e128_ref[...], preferred_element_type=F32)

        xs = _silu(_causal_conv(xext_ref, xtail_ref, x_ref[rows, :].astype(F32), cwx_ref, cbx_ref))
        bc = _silu(_causal_conv(bcext_ref, bctail_ref, bc_ref[rows, :].astype(F32), cwbc_ref, cbbc_ref))

        cs_last = cs_e[L - 1:L, :]
        x_dt = xs * dt_e
        x_dt_b = x_dt.astype(BF16)
        x_end_b = (x_dt * jnp.exp(cs_last - cs_e)).astype(BF16)
        in_decay = jnp.exp(cs_e)
        chunk_decay = jnp.exp(cs_last)

        lane_l = lax.broadcasted_iota(jnp.int32, (L, 2 * SSD_HEAD_DIM), 1)
        first_head = lane_l < SSD_HEAD_DIM
        ssq = jnp.zeros((L, 1), F32)
        for g in range(SSD_GROUPS):
            gs = slice(g * GROUP_COLS, (g + 1) * GROUP_COLS)
            b_g = bc[:, g * SSD_STATE:(g + 1) * SSD_STATE].astype(BF16)
            c_g = bc[:, (SSD_GROUPS + g) * SSD_STATE:(SSD_GROUPS + g + 1) * SSD_STATE].astype(BF16)
            scores = lax.dot_general(c_g, b_g, (((1,), (1,)), ((), ())),
                                     preferred_element_type=F32)
            heads_per_group = SSD_HEADS // SSD_GROUPS
            y_parts = []
            for pair in range(heads_per_group // 2):
                ms = []
                for r in range(2):
                    h = g * heads_per_group + 2 * pair + r
                    seg = cs_col[:, h * L:(h + 1) * L] - cs_t[h:h + 1, :]
                    decay = jnp.exp(jnp.where(causal, seg, -jnp.inf))
                    ms.append((scores * decay).astype(BF16))
                lhs = jnp.concatenate(ms, axis=1)
                c0 = g * GROUP_COLS + pair * 2 * SSD_HEAD_DIM
                xp = x_dt_b[:, c0:c0 + 2 * SSD_HEAD_DIM]
                zero = jnp.zeros_like(xp)
                rhs = jnp.concatenate([jnp.where(first_head, xp, zero),
                                       jnp.where(first_head, zero, xp)], axis=0)
                y_parts.append(jnp.dot(lhs, rhs, preferred_element_type=F32))
            y_diag = jnp.concatenate(y_parts, axis=1)

            st = state_ref[:, gs]
            y_off = jnp.dot(c_g, st.astype(BF16), preferred_element_type=F32) * in_decay[:, gs]
            new_st = lax.dot_general(b_g, x_end_b[:, gs], (((0,), (0,)), ((), ())),
                                     preferred_element_type=F32)
            state_ref[:, gs] = chunk_decay[:, gs] * st + new_st

            y = y_diag + y_off + xs[:, gs] * dskip_ref[:, gs]
            y = y * _silu(z_ref[rows, gs].astype(F32))
            ssq = ssq + jnp.sum(y * y, axis=-1, keepdims=True)
            y_ref[:, gs] = y

        inv = lax.rsqrt(ssq * (1.0 / D_INNER) + NORM_EPS)
        o_ref[rows, :] = (y_ref[...] * inv * nw_ref[...]).astype(o_ref.dtype)

    for cc in range(SSD_CHUNKS_PER_STEP):
        one_chunk(slice(cc * CHUNK, (cc + 1) * CHUNK))


def _ssd_mixer(proj, dt_raw, conv_w, conv_b, dt_bias, a_log, d_skip, norm_w, *, batch, seq):
    rows = SSD_CHUNKS_PER_STEP * CHUNK
    nc = seq // rows
    t = batch * seq
    lanes = jnp.arange(DT_LANES)
    valid = lanes < DT_COPIES * SSD_HEADS
    head_of_lane = lanes % SSD_HEADS

    def tile_heads(v):
        return jnp.pad(v.astype(F32), (0, DT_LANES - SSD_HEADS)).reshape(1, DT_LANES)

    def expand(width):
        cols = jnp.arange(SSD_HEADS * width) // width
        return ((head_of_lane[:, None] == cols[None, :]) & valid[:, None]).astype(BF16)

    e64 = expand(SSD_HEAD_DIM)
    e128 = expand(CHUNK)
    d_skip_e = jnp.repeat(d_skip, SSD_HEAD_DIM).reshape(1, D_INNER)

    tok = lambda b, c: b * nc + c
    const = lambda b, c: (0, 0)
    full = lambda a: pl.BlockSpec(a.shape, const)
    cwx, cwbc = conv_w[:, :D_INNER], conv_w[:, D_INNER:]
    cbx, cbbc = conv_b[:D_INNER].reshape(1, -1), conv_b[D_INNER:].reshape(1, -1)
    dtb, alog = tile_heads(dt_bias), tile_heads(a_log)
    nw = norm_w.reshape(1, D_INNER)
    params = [cwx, cbx, cwbc, cbbc, dtb, alog, d_skip_e, nw, e64, e128]
    return pl.pallas_call(
        _ssd_kernel,
        grid=(batch, nc),
        in_specs=[
            pl.BlockSpec((rows, D_INNER), lambda b, c: (tok(b, c), 0)),
            pl.BlockSpec((rows, D_INNER), lambda b, c: (tok(b, c), 4)),
            pl.BlockSpec((rows, BC_DIM), lambda b, c: (tok(b, c), 10)),
            pl.BlockSpec((rows, DT_LANES), lambda b, c: (tok(b, c), 0)),
        ] + [full(p) for p in params],
        out_specs=pl.BlockSpec((rows, D_INNER), lambda b, c: (tok(b, c), 0)),
        out_shape=jax.ShapeDtypeStruct((t, D_INNER), BF16),
        scratch_shapes=[
            pltpu.VMEM((SSD_STATE, D_INNER), F32),
            pltpu.VMEM((CONV_HALO, D_INNER), F32),
            pltpu.VMEM((CONV_HALO, BC_DIM), F32),
            pltpu.VMEM((CONV_HALO + CHUNK, D_INNER), F32),
            pltpu.VMEM((CONV_HALO + CHUNK, BC_DIM), F32),
            pltpu.VMEM((CHUNK, D_INNER), F32),
        ],
        compiler_params=_params(("parallel", "arbitrary")),
        name="ssd_mixer",
    )(proj, proj, proj, dt_raw, *params)


def _sgu_kernel(g_ref, u_ref, v_ref, lnw_ref, lnb_ref, ws_ref, bias_ref, o_ref, wtri_ref):
    @pl.when(pl.program_id(0) == 0)
    def _():
        row = lax.broadcasted_iota(jnp.int32, (CHUNK, CHUNK), 0)
        col = lax.broadcasted_iota(jnp.int32, (CHUNK, CHUNK), 1)
        for g in range(SGU_GROUPS):
            wtri_ref[g] = jnp.where(row >= col, ws_ref[g], 0.0).astype(BF16)

    v = _gelu_tanh(v_ref[...].astype(F32))
    mu = jnp.mean(v, axis=-1, keepdims=True)
    vc = v - mu
    var = jnp.mean(vc * vc, axis=-1, keepdims=True)
    vn = (vc * lax.rsqrt(var + NORM_EPS) * lnw_ref[...] + lnb_ref[...]).astype(BF16)
    for g in range(SGU_GROUPS):
        gs = slice(g * SGU_GROUP_DIM, (g + 1) * SGU_GROUP_DIM)
        mixed = jnp.dot(wtri_ref[g], vn[:, gs], preferred_element_type=F32) + bias_ref[:, gs]
        u = _gelu_tanh(u_ref[:, gs].astype(F32))
        o_ref[:, gs] = (u * mixed * _silu(g_ref[:, gs].astype(F32))).astype(o_ref.dtype)


def _sgu_mixer(proj, ln_w, ln_b, w_s, b_s):
    t = proj.shape[0]
    bias = jnp.repeat(b_s.T, SGU_GROUP_DIM, axis=1)
    const2 = lambda i: (0, 0)
    return pl.pallas_call(
        _sgu_kernel,
        grid=(t // CHUNK,),
        in_specs=[
            pl.BlockSpec((CHUNK, D_INNER), lambda i: (i, 1)),
            pl.BlockSpec((CHUNK, D_INNER), lambda i: (i, 2)),
            pl.BlockSpec((CHUNK, D_INNER), lambda i: (i, 3)),
            pl.BlockSpec((1, D_INNER), const2),
            pl.BlockSpec((1, D_INNER), const2),
            pl.BlockSpec(w_s.shape, lambda i: (0, 0, 0)),
            pl.BlockSpec((CHUNK, D_INNER), const2),
        ],
        out_specs=pl.BlockSpec((CHUNK, D_INNER), lambda i: (i, 0)),
        out_shape=jax.ShapeDtypeStruct((t, D_INNER), BF16),
        scratch_shapes=[pltpu.VMEM((SGU_GROUPS, CHUNK, CHUNK), BF16)],
        compiler_params=_params(("arbitrary",)),
        name="sgu_mixer",
    )(proj, proj, proj, ln_w.reshape(1, -1), ln_b.reshape(1, -1), w_s, bias)


def _attn_kernel(q_ref, k_ref, v_ref, g_ref, o_ref,
                 acc_ref, carry_ref, za_ref, zb_ref, wa_ref, wb_ref):
    qi = pl.program_id(2)
    tq, tk, nsub = ATTN_TQ, ATTN_TK, ATTN_SUB
    kb = tk * nsub
    q = q_ref[...]
    srow = lax.broadcasted_iota(jnp.int32, (tk, 2 * tk), 0)
    scol = lax.broadcasted_iota(jnp.int32, (tk, 2 * tk), 1)
    later2 = jnp.where((scol & (tk - 1)) > srow, 1.0, 0.0).astype(BF16)

    def scores(j):
        ks = pl.multiple_of(j * kb, kb)
        return lax.dot_general(k_ref[pl.ds(ks, kb), :], q, (((1,), (1,)), ((), ())),
                               preferred_element_type=F32)

    def weights(z, carry, masked):
        sp = jnp.maximum(z, 0.0) + jnp.log2(1.0 + jnp.exp2(-jnp.abs(z)))
        if masked:
            m = (lax.broadcasted_iota(jnp.int32, (kb, tq), 0)
                 < lax.broadcasted_iota(jnp.int32, (kb, tq), 1))
            sp = jnp.where(m, sp, 0.0)
        hi = sp.astype(BF16)
        lo = (sp - hi.astype(F32)).astype(BF16)
        afts, tots = [], []
        for r in range(nsub):
            rs = slice(r * tk, (r + 1) * tk)
            hl = jnp.concatenate([hi[rs], lo[rs]], axis=0)
            aft = jnp.dot(later2, hl, preferred_element_type=F32)
            afts.append(aft)
            tots.append(aft[0:1, :] + sp[r * tk:r * tk + 1, :])
        ws = [None] * nsub
        for r in reversed(range(nsub)):
            rs = slice(r * tk, (r + 1) * tk)
            ws[r] = jnp.exp2((z[rs] - sp[rs]) - afts[r] - carry).astype(BF16)
            carry = carry + tots[r]
        w = jnp.concatenate(ws, axis=0)
        if masked:
            w = jnp.where(m, w, jnp.zeros_like(w))
        return carry, w

    def add_values(j, w):
        ks = pl.multiple_of(j * kb, kb)
        acc_ref[...] += lax.dot_general(v_ref[pl.ds(ks, kb), :], w, (((0,), (0,)), ((), ())),
                                        preferred_element_type=F32)

    def pipelined(j, z_cur_ref, z_next_ref, w_prev_ref, w_cur_ref):
        add_values(j + 1, w_prev_ref[...])
        z_next_ref[...] = scores(jnp.maximum(j - 1, 0))
        carry, w = weights(z_cur_ref[...], carry_ref[...], False)
        carry_ref[...] = carry
        w_cur_ref[...] = w

    acc_ref[...] = jnp.zeros_like(acc_ref)
    odd = qi % 2
    z_diag = scores(qi)
    za_ref[...] = scores(jnp.maximum(qi - 1, 0))
    carry, w_diag = weights(z_diag, jnp.zeros((1, tq), F32), True)
    carry_ref[...] = carry
    wb_ref[...] = w_diag

    @pl.when(odd == 1)
    def _():
        add_values(qi, wb_ref[...])
        carry, w = weights(za_ref[...], carry_ref[...], False)
        carry_ref[...] = carry
        wb_ref[...] = w
        za_ref[...] = scores(jnp.maximum(qi - 2, 0))

    top = qi - odd

    def pair(p, _):
        j = top - 1 - 2 * p
        pipelined(j, za_ref, zb_ref, wb_ref, wa_ref)
        pipelined(j - 1, zb_ref, za_ref, wa_ref, wb_ref)
        return 0

    lax.fori_loop(0, top // 2, pair, 0)
    add_values(0, wb_ref[...])
    o_ref[...] = (acc_ref[...].T * _silu(g_ref[...].astype(F32))).astype(o_ref.dtype)


def _attention(qkvg, *, batch, seq):
    assert ATTN_TQ == ATTN_TK * ATTN_SUB
    nq = seq // ATTN_TQ
    t = batch * seq
    return pl.pallas_call(
        _attn_kernel,
        grid=(batch, SB_HEADS, nq),
        in_specs=[
            pl.BlockSpec((ATTN_TQ, SB_HEAD_DIM), lambda b, h, i: (b * nq + i, h)),
            pl.BlockSpec((seq, SB_HEAD_DIM), lambda b, h, i: (b, SB_HEADS + h)),
            pl.BlockSpec((seq, SB_HEAD_DIM), lambda b, h, i: (b, 2 * SB_HEADS + h)),
            pl.BlockSpec((ATTN_TQ, SB_HEAD_DIM), lambda b, h, i: (b * nq + i, 3 * SB_HEADS + h)),
        ],
        out_specs=pl.BlockSpec((ATTN_TQ, SB_HEAD_DIM), lambda b, h, i: (b * nq + i, h)),
        out_shape=jax.ShapeDtypeStruct((t, SB_HEADS * SB_HEAD_DIM), BF16),
        scratch_shapes=[
            pltpu.VMEM((SB_HEAD_DIM, ATTN_TQ), F32),
            pltpu.VMEM((1, ATTN_TQ), F32),
            pltpu.VMEM((ATTN_TQ, ATTN_TQ), F32),
            pltpu.VMEM((ATTN_TQ, ATTN_TQ), F32),
            pltpu.VMEM((ATTN_TQ, ATTN_TQ), BF16),
            pltpu.VMEM((ATTN_TQ, ATTN_TQ), BF16),
        ],
        compiler_params=_params(("parallel", "parallel", "arbitrary")),
        name="stickbreak_attn",
    )(qkvg, qkvg, qkvg, qkvg)


EVEN_ALIGNED = 2 * D_INNER + BC_DIM
EVEN_AFTER_DT = EVEN_ALIGNED + SSD_HEADS


def _even_layer(h, norm_w, ev_w_in_t, ev_w_out, layer, conv_w, conv_b, dt_bias, a_log,
                d_skip, ssd_norm_w, sgu_ln_w, sgu_ln_b, sgu_w, sgu_b, *, batch, seq, in_place):
    tm, tn = IN_PROJ_TILE
    tiles = lambda cols: cols // tn
    proj, dt_raw = _in_proj(
        h, norm_w, ev_w_in_t, layer, tm=tm, tn=tn, transposed=True,
        segments=[(tiles(D_INNER), 0, 0),
                  (tiles(D_INNER + BC_DIM), D_INNER, tiles(4 * D_INNER)),
                  (tiles(3 * D_INNER), EVEN_AFTER_DT, tiles(D_INNER))],
        dt_start=EVEN_ALIGNED)
    y_a = _ssd_mixer(proj, dt_raw, conv_w, conv_b, dt_bias, a_log, d_skip, ssd_norm_w,
                     batch=batch, seq=seq)
    y_b = _sgu_mixer(proj, sgu_ln_w, sgu_ln_b, sgu_w, sgu_b)
    return _out_proj([y_a, y_b], [(ev_w_out, layer, 0), (ev_w_out, layer, 1)], h, None,
                     tm=EVEN_OUT_TILE[0], tn=EVEN_OUT_TILE[1], in_place=in_place)


def _odd_layer(h, norm_w, od_w_in, od_w_out, layer, final_w, *, batch, seq):
    scale = SB_HEAD_DIM ** -0.5 * math.log2(math.e)
    tm, tn = IN_PROJ_TILE
    qkvg = _in_proj(h, norm_w, od_w_in, layer, tm=tm, tn=tn,
                    segments=[(4 * D_INNER // tn, 0, 0)], scaled_tiles=D_INNER // tn, scale=scale)
    y = _attention(qkvg, batch=batch, seq=seq)
    tm, tn = ODD_OUT_TILE if final_w is None else FINAL_OUT_TILE
    return _out_proj([y], [(od_w_out, layer, 0)], h, final_w, tm=tm, tn=tn, in_place=True)


def kernel(x, norm_w, final_norm_w, ev_w_in, ev_conv_w, ev_conv_b, ev_dt_bias, ev_a_log,
           ev_d_skip, ev_ssd_norm_w, ev_sgu_ln_w, ev_sgu_ln_b, ev_sgu_w, ev_sgu_b, ev_w_out,
           od_w_in, od_w_out):
    batch, seq, d = x.shape
    depth = norm_w.shape[0]
    h = x.reshape(batch * seq, d)
    ev_w_in_t = jnp.swapaxes(ev_w_in, 1, 2)
    for layer in range(depth):
        i = layer // 2
        if layer % 2 == 0:
            h = _even_layer(h, norm_w[layer], ev_w_in_t, ev_w_out, i, ev_conv_w[i],
                            ev_conv_b[i], ev_dt_bias[i], ev_a_log[i], ev_d_skip[i],
                            ev_ssd_norm_w[i], ev_sgu_ln_w[i], ev_sgu_ln_b[i], ev_sgu_w[i],
                            ev_sgu_b[i], batch=batch, seq=seq, in_place=layer > 0)
        else:
            fw = final_norm_w if layer == depth - 1 else None
            h = _odd_layer(h, norm_w[layer], od_w_in, od_w_out, i, fw, batch=batch, seq=seq)
    if depth % 2 == 1:
        raise NotImplementedError("final RMSNorm is fused into an odd last layer")
    return h.reshape(batch, seq, d)
```
